```python
import math
import jax, jax.numpy as jnp
from jax import lax
import numpy as np

D_MODEL = 1024
BATCH = 8
SEQ = 2048
DEPTH = 4
DEC_BATCH = 128
DEC_SEQ = 4
PAST_LEN = 16384
PAGE_SIZE = 128

D_PLE = 256
D_POOL = D_MODEL // 2
D_SSM = D_MODEL // 2
D_MIX = D_POOL + D_SSM
POOL_WINDOWS = (2, 4, 8, 16)
N_POOL_GROUPS = len(POOL_WINDOWS)
POOL_GW = D_POOL // N_POOL_GROUPS
POOL_BUF = max(POOL_WINDOWS) - 1
SSM_H = 16
SSM_GROUPS = D_SSM // SSM_H
SSM_STATE = 64
DT_MIN = 1e-3
DT_MAX = 1e-1
N_EXPERTS = 64
TOP_K = 8
N_EXPERT_GROUPS = 8
TOPK_GROUPS = 4
D_EXPERT = 128
D_SHARED = 128
ROUTED_SCALE = 2.5
MOE_BLOCK = 512
ALPHA = (2.0 * DEPTH) ** 0.25
BETA = (8.0 * DEPTH) ** -0.25
LN_EPS = 1e-5

kernel_name = 'hymba_pool_s5_moe_deepnorm_step'


def _layernorm(x, g, b):
    xf = x.astype(jnp.float32)
    mu = jnp.mean(xf, axis=-1, keepdims=True)
    xc = xf - mu
    var = jnp.mean(xc * xc, axis=-1, keepdims=True)
    y = xc * lax.rsqrt(var + LN_EPS)
    return (y * g.astype(jnp.float32) + b.astype(jnp.float32)).astype(x.dtype)


def _pool_mixer(u, buf, start_pos, w_pool, pool_scale):
    L = u.shape[1]
    xcat = jnp.concatenate([buf.astype(u.dtype), u], axis=1)
    new_buf = xcat[:, -POOL_BUF:]
    xf = xcat.astype(jnp.float32)
    cs = jnp.concatenate([jnp.zeros_like(xf[:, :1]), jnp.cumsum(xf, axis=1)], axis=1)
    pos = start_pos + jnp.arange(L, dtype=jnp.int32)
    outs = []
    for g, w in enumerate(POOL_WINDOWS):
        lo, hi = g * POOL_GW, (g + 1) * POOL_GW
        c_g = cs[..., lo:hi]
        wsum = c_g[:, POOL_BUF + 1:POOL_BUF + 1 + L] - c_g[:, POOL_BUF + 1 - w:POOL_BUF + 1 - w + L]
        cnt = jnp.minimum(pos + 1, w).astype(jnp.float32)[None, :, None]
        pooled = wsum / cnt - xf[:, POOL_BUF:, lo:hi]
        outs.append(jnp.einsum('blc,cd->bld', pooled, w_pool[g].astype(jnp.float32)))
    out = jnp.concatenate(outs, axis=-1) * pool_scale.astype(jnp.float32)
    return out.astype(u.dtype), new_buf


def _complex_affine_combine(e1, e2):
    a1r, a1i, b1r, b1i = e1
    a2r, a2i, b2r, b2i = e2
    ar = a2r * a1r - a2i * a1i
    ai = a2r * a1i + a2i * a1r
    br = a2r * b1r - a2i * b1i + b2r
    bi = a2r * b1i + a2i * b1r + b2i
    return (ar, ai, br, bi)


def _s5_mixer(u, h0_re, h0_im, lw):
    Bsz, L, _ = u.shape
    uf = u.astype(jnp.float32).reshape(Bsz, L, SSM_GROUPS, SSM_H)
    a_re = lw['ssm_a_re'].astype(jnp.float32)
    a_im = lw['ssm_a_im'].astype(jnp.float32)
    dt = jnp.exp(lw['ssm_log_dt'].astype(jnp.float32))[:, None]
    mag = jnp.exp(a_re * dt)
    lam_re = mag * jnp.cos(a_im * dt)
    lam_im = mag * jnp.sin(a_im * dt)
    den = a_re * a_re + a_im * a_im
    nr = lam_re - 1.0
    ni = lam_im
    f_re = (nr * a_re + ni * a_im) / den
    f_im = (ni * a_re - nr * a_im) / den
    b_re = lw['ssm_b_re'].astype(jnp.float32)
    b_im = lw['ssm_b_im'].astype(jnp.float32)
    bbar_re = f_re[..., None] * b_re - f_im[..., None] * b_im
    bbar_im = f_re[..., None] * b_im + f_im[..., None] * b_re
    bu_re = jnp.einsum('blgh,gph->blgp', uf, bbar_re)
    bu_im = jnp.einsum('blgh,gph->blgp', uf, bbar_im)
    h0r = h0_re.astype(jnp.float32)
    h0i = h0_im.astype(jnp.float32)
    bu_re = bu_re.at[:, 0].add(lam_re * h0r - lam_im * h0i)
    bu_im = bu_im.at[:, 0].add(lam_re * h0i + lam_im * h0r)
    ar_b = jnp.broadcast_to(lam_re, bu_re.shape)
    ai_b = jnp.broadcast_to(lam_im, bu_re.shape)
    _, _, hr, hi = lax.associative_scan(_complex_affine_combine, (ar_b, ai_b, bu_re, bu_im), axis=1)
    c_re = lw['ssm_c_re'].astype(jnp.float32)
    c_im = lw['ssm_c_im'].astype(jnp.float32)
    d_skip = lw['ssm_d'].astype(jnp.float32).reshape(SSM_GROUPS, SSM_H)
    y = (jnp.einsum('blgp,ghp->blgh', hr, c_re) - jnp.einsum('blgp,ghp->blgh', hi, c_im)
         + d_skip * uf).reshape(Bsz, L, D_SSM)
    z = jax.nn.gelu(y)
    out = z * jax.nn.sigmoid(z @ lw['w_glu'].astype(jnp.float32) + lw['b_glu'].astype(jnp.float32))
    return out.astype(u.dtype), hr[:, -1], hi[:, -1]


def _route(xb, w_router, b_router):
    T = xb.shape[0]
    scores = jax.nn.sigmoid((xb @ w_router).astype(jnp.float32))
    sel = scores + b_router.astype(jnp.float32)
    per_group = N_EXPERTS // N_EXPERT_GROUPS
    grp = sel.reshape(T, N_EXPERT_GROUPS, per_group)
    gscore = jnp.sum(lax.top_k(grp, 2)[0], axis=-1)
    _, gidx = lax.top_k(gscore, TOPK_GROUPS)
    gmask = jnp.sum(jax.nn.one_hot(gidx, N_EXPERT_GROUPS, dtype=jnp.float32), axis=1)
    emask = jnp.repeat(gmask, per_group, axis=1)
    masked = jnp.where(emask > 0, sel, -jnp.inf)
    _, eidx = lax.top_k(masked, TOP_K)
    w = jnp.take_along_axis(scores, eidx, axis=1)
    w = w / jnp.sum(w, axis=-1, keepdims=True) * ROUTED_SCALE
    return jnp.sum(jax.nn.one_hot(eidx, N_EXPERTS, dtype=jnp.float32) * w[..., None], axis=1)


def _moe(x, lw):
    Bsz, L, D = x.shape
    T = Bsz * L
    nb = -(-T // MOE_BLOCK)
    xt = jnp.pad(x.reshape(T, D), ((0, nb * MOE_BLOCK - T), (0, 0))).reshape(nb, MOE_BLOCK, D)

    def block(xb):
        gate = _route(xb, lw['w_router'], lw['b_router'])
        hg = jnp.einsum('td,edh->teh', xb, lw['w_gate'])
        hu = jnp.einsum('td,edh->teh', xb, lw['w_up'])
        hdn = jax.nn.silu(hg) * hu * gate[..., None].astype(hg.dtype)
        routed = jnp.einsum('teh,ehd->td', hdn, lw['w_down'])
        shared = (jax.nn.silu(xb @ lw['ws_gate']) * (xb @ lw['ws_up'])) @ lw['ws_down']
        return routed + shared

    y = lax.map(block, xt).reshape(nb * MOE_BLOCK, D)[:T]
    return y.reshape(Bsz, L, D)


def _layer(h, p, pool_buf, h_re, h_im, start_pos, lw):
    u = h @ lw['w_in']
    pool_out, new_buf = _pool_mixer(u[..., :D_POOL], pool_buf, start_pos, lw['w_pool'], lw['pool_scale'])
    ssm_out, new_re, new_im = _s5_mixer(u[..., D_POOL:], h_re, h_im, lw)
    mix = jnp.concatenate([pool_out, ssm_out], axis=-1) @ lw['w_out']
    h = _layernorm(ALPHA * h + mix, lw['ln1_g'], lw['ln1_b'])
    ffn = _moe(h, lw)
    gate = jax.nn.sigmoid((h @ lw['w_pg'] + lw['b_pg']).astype(jnp.float32)).astype(h.dtype)
    ple = gate * (p @ lw['w_pp'])
    h = _layernorm(ALPHA * h + ffn + ple, lw['ln2_g'], lw['ln2_b'])
    return h, new_buf, new_re, new_im


def setup_inputs(seed: int = 0) -> dict:
    key = jax.random.key(seed)
    ks = jax.random.split(key, 40)

    def nrm(k, shape, scale=1.0):
        return jax.random.normal(k, shape, jnp.float32) * scale

    a_im_init = jnp.pi * jnp.arange(SSM_STATE, dtype=jnp.float32)
    return {
        'x_prompt': nrm(ks[0], (BATCH, SEQ, D_MODEL)),
        'x_sample': nrm(ks[1], (DEC_BATCH, DEC_SEQ, D_MODEL)),
        'state_pool': nrm(ks[2], (DEPTH, DEC_BATCH, POOL_BUF, D_POOL)),
        'state_ssm_re': nrm(ks[3], (DEPTH, DEC_BATCH, SSM_GROUPS, SSM_STATE), 0.05),
        'state_ssm_im': nrm(ks[4], (DEPTH, DEC_BATCH, SSM_GROUPS, SSM_STATE), 0.05),
        'p_prompt': nrm(ks[5], (DEPTH, BATCH, SEQ, D_PLE)),
        'p_sample': nrm(ks[6], (DEPTH, DEC_BATCH, DEC_SEQ, D_PLE)),
        'ln_in_g': 1.0 + nrm(ks[7], (D_MODEL,), 0.02),
        'ln_in_b': nrm(ks[8], (D_MODEL,), 0.02),
        'w_in': nrm(ks[9], (DEPTH, D_MODEL, D_MIX), D_MODEL ** -0.5),
        'w_pool': nrm(ks[10], (DEPTH, N_POOL_GROUPS, POOL_GW, POOL_GW), POOL_GW ** -0.5),
        'pool_scale': 1.0 + nrm(ks[11], (DEPTH, D_POOL), 0.02),
        'ssm_a_re': -0.5 + nrm(ks[12], (DEPTH, SSM_GROUPS, SSM_STATE), 0.01),
        'ssm_a_im': a_im_init + nrm(ks[13], (DEPTH, SSM_GROUPS, SSM_STATE), 0.01),
        'ssm_log_dt': jax.random.uniform(ks[14], (DEPTH, SSM_GROUPS), jnp.float32, math.log(DT_MIN), math.log(DT_MAX)),
        'ssm_b_re': nrm(ks[15], (DEPTH, SSM_GROUPS, SSM_STATE, SSM_H), (2.0 * SSM_H) ** -0.5),
        'ssm_b_im': nrm(ks[16], (DEPTH, SSM_GROUPS, SSM_STATE, SSM_H), (2.0 * SSM_H) ** -0.5),
        'ssm_c_re': nrm(ks[17], (DEPTH, SSM_GROUPS, SSM_H, SSM_STATE), SSM_STATE ** -0.5),
        'ssm_c_im': nrm(ks[18], (DEPTH, SSM_GROUPS, SSM_H, SSM_STATE), SSM_STATE ** -0.5),
        'ssm_d': nrm(ks[19], (DEPTH, D_SSM)),
        'w_glu': nrm(ks[20], (DEPTH, D_SSM, D_SSM), D_SSM ** -0.5),
        'b_glu': nrm(ks[21], (DEPTH, D_SSM), 0.02),
        'w_out': nrm(ks[22], (DEPTH, D_MIX, D_MODEL), BETA * D_MIX ** -0.5),
        'ln1_g': 1.0 + nrm(ks[23], (DEPTH, D_MODEL), 0.02),
        'ln1_b': nrm(ks[24], (DEPTH, D_MODEL), 0.02),
        'w_router': nrm(ks[25], (DEPTH, D_MODEL, N_EXPERTS), D_MODEL ** -0.5),
        'b_router': nrm(ks[26], (DEPTH, N_EXPERTS), 0.01),
        'w_gate': nrm(ks[27], (DEPTH, N_EXPERTS, D_MODEL, D_EXPERT), D_MODEL ** -0.5),
        'w_up': nrm(ks[28], (DEPTH, N_EXPERTS, D_MODEL, D_EXPERT), D_MODEL ** -0.5),
        'w_down': nrm(ks[29], (DEPTH, N_EXPERTS, D_EXPERT, D_MODEL), BETA * D_EXPERT ** -0.5),
        'ws_gate': nrm(ks[30], (DEPTH, D_MODEL, D_SHARED), D_MODEL ** -0.5),
        'ws_up': nrm(ks[31], (DEPTH, D_MODEL, D_SHARED), D_MODEL ** -0.5),
        'ws_down': nrm(ks[32], (DEPTH, D_SHARED, D_MODEL), BETA * D_SHARED ** -0.5),
        'w_pg': nrm(ks[33], (DEPTH, D_MODEL, D_MODEL), D_MODEL ** -0.5),
        'b_pg': nrm(ks[34], (DEPTH, D_MODEL), 0.02),
        'w_pp': nrm(ks[35], (DEPTH, D_PLE, D_MODEL), BETA * D_PLE ** -0.5),
        'ln2_g': 1.0 + nrm(ks[36], (DEPTH, D_MODEL), 0.02),
        'ln2_b': nrm(ks[37], (DEPTH, D_MODEL), 0.02),
    }


def reference(x_prompt, x_sample, state_pool, state_ssm_re, state_ssm_im, p_prompt, p_sample,
              ln_in_g, ln_in_b, w_in, w_pool, pool_scale, ssm_a_re, ssm_a_im, ssm_log_dt,
              ssm_b_re, ssm_b_im, ssm_c_re, ssm_c_im, ssm_d, w_glu, b_glu, w_out, ln1_g, ln1_b,
              w_router, b_router, w_gate, w_up, w_down, ws_gate, ws_up, ws_down,
              w_pg, b_pg, w_pp, ln2_g, ln2_b):
    hp = _layernorm(x_prompt, ln_in_g, ln_in_b)
    hs = _layernorm(x_sample, ln_in_g, ln_in_b)
    n_prompt = x_prompt.shape[0]
    buf0 = jnp.zeros((n_prompt, POOL_BUF, D_POOL), x_prompt.dtype)
    h0 = jnp.zeros((n_prompt, SSM_GROUPS, SSM_STATE), jnp.float32)
    pool_p, re_p, im_p, pool_s, re_s, im_s = [], [], [], [], [], []
    for i in range(DEPTH):
        lw = {
            'w_in': w_in[i], 'w_pool': w_pool[i], 'pool_scale': pool_scale[i],
            'ssm_a_re': ssm_a_re[i], 'ssm_a_im': ssm_a_im[i], 'ssm_log_dt': ssm_log_dt[i],
            'ssm_b_re': ssm_b_re[i], 'ssm_b_im': ssm_b_im[i], 'ssm_c_re': ssm_c_re[i],
            'ssm_c_im': ssm_c_im[i], 'ssm_d': ssm_d[i], 'w_glu': w_glu[i], 'b_glu': b_glu[i],
            'w_out': w_out[i], 'ln1_g': ln1_g[i], 'ln1_b': ln1_b[i],
            'w_router': w_router[i], 'b_router': b_router[i], 'w_gate': w_gate[i],
            'w_up': w_up[i], 'w_down': w_down[i], 'ws_gate': ws_gate[i], 'ws_up': ws_up[i],
            'ws_down': ws_down[i], 'w_pg': w_pg[i], 'b_pg': b_pg[i], 'w_pp': w_pp[i],
            'ln2_g': ln2_g[i], 'ln2_b': ln2_b[i],
        }
        hp, bp, rp, ipm = _layer(hp, p_prompt[i], buf0, h0, h0, 0, lw)
        hs, bs, rs, ism = _layer(hs, p_sample[i], state_pool[i], state_ssm_re[i], state_ssm_im[i], PAST_LEN, lw)
        pool_p.append(bp)
        re_p.append(rp)
        im_p.append(ipm)
        pool_s.append(bs)
        re_s.append(rs)
        im_s.append(ism)
    return (hp, hs, jnp.stack(pool_p), jnp.stack(re_p), jnp.stack(im_p),
            jnp.stack(pool_s), jnp.stack(re_s), jnp.stack(im_s))
```

```python
import functools

import jax
import jax.numpy as jnp
from jax import lax
from jax.experimental import pallas as pl
from jax.experimental.pallas import tpu as pltpu

F32 = jnp.float32
BF16 = jnp.bfloat16

POOL_WINDOWS = (2, 4, 8, 16)
POOL_BUF = max(POOL_WINDOWS) - 1
SSM_H = 16
SSM_STATE = 64
N_EXPERT_GROUPS = 8
EXPERTS_PER_GROUP = 8
TOPK_GROUPS = 4
TOP_K = 8
ROUTED_SCALE = 2.5
PAST_LEN = 16384
LN_EPS = 1e-5

LANES = 128
SUBLANES = 8
SCAN_LANES = 512
MIXER_ROWS = 512
FFN_ROWS = 1024
FFN_PAIRS = 4
VMEM_LIMIT = 56 * 1024 * 1024


def _dot(a, b):
    return jnp.dot(a, b, preferred_element_type=F32)


def _layernorm(x, g, b):
    mu = jnp.mean(x, axis=-1, keepdims=True)
    xc = x - mu
    var = jnp.mean(xc * xc, axis=-1, keepdims=True)
    return xc * lax.rsqrt(var + LN_EPS) * g + b


def _s5_prep_body(are_ref, aim_ref, ldt_ref, bre_ref, bim_ref,
                  lre_ref, lim_ref, bbre_ref, bbim_ref):
    a_re = are_ref[...]
    a_im = aim_ref[...]
    dt = jnp.exp(ldt_ref[...])
    mag = jnp.exp(a_re * dt)
    lam_re = mag * jnp.cos(a_im * dt)
    lam_im = mag * jnp.sin(a_im * dt)
    den = a_re * a_re + a_im * a_im
    nr = lam_re - 1.0
    ni = lam_im
    f_re = ((nr * a_re + ni * a_im) / den)[:, None, :]
    f_im = ((ni * a_re - nr * a_im) / den)[:, None, :]
    b_re = bre_ref[...]
    b_im = bim_ref[...]
    lre_ref[...] = lam_re
    lim_ref[...] = lam_im
    bbre_ref[...] = f_re * b_re - f_im * b_im
    bbim_ref[...] = f_re * b_im + f_im * b_re


def _s5_prep(a_re, a_im, log_dt_b, bt_re, bt_im):
    n, p = a_re.shape
    return pl.pallas_call(
        _s5_prep_body,
        out_shape=(jax.ShapeDtypeStruct((n, p), F32), jax.ShapeDtypeStruct((n, p), F32),
                   jax.ShapeDtypeStruct(bt_re.shape, F32), jax.ShapeDtypeStruct(bt_im.shape, F32)),
        name="s5_prep",
    )(a_re, a_im, log_dt_b, bt_re, bt_im)


def _mixer_body(nblk, batch, lt, start_pos, alpha, first,
                h_ref, lng_ref, lnb_ref, win_ref, wpool_ref, pscale_ref, bblk_ref,
                lre_ref, lim_ref, cblk_ref, dskip_ref, wglu_ref, bglu_ref, wout_ref,
                ln1g_ref, ln1b_ref, pool0_ref, s0re_ref, s0im_ref,
                out_ref, opool_ref, osre_ref, osim_ref,
                ext_ref, bur_ref, bui_ref, sre_ref, sim_ref):
    i = pl.program_id(0)
    rows = lt * batch
    hist = POOL_BUF * batch
    d_pool = ext_ref.shape[1]
    n_state = bur_ref.shape[1]
    gw = d_pool // len(POOL_WINDOWS)
    n_chunk = bblk_ref.shape[0]
    ch_in = bblk_ref.shape[1]
    ch_st = bblk_ref.shape[2] // 2

    @pl.when(i == 0)
    def _():
        ext_ref[0:hist, :] = pool0_ref[...]
        sre_ref[...] = s0re_ref[...]
        sim_ref[...] = s0im_ref[...]

    x = h_ref[...]
    if first:
        x = _layernorm(x, lng_ref[...], lnb_ref[...])
    u = _dot(x.astype(BF16), win_ref[...])
    u_pool = u[:, :d_pool]
    u_ssm = u[:, d_pool:]
    ext_ref[hist:hist + rows, :] = u_pool

    t_loc = lax.broadcasted_iota(jnp.int32, (lt, batch, gw), 0).reshape(rows, gw)
    pos1 = t_loc + (start_pos + 1) + i * lt
    pool_outs = []
    for g, w in enumerate(POOL_WINDOWS):
        s = ext_ref[:, g * gw:(g + 1) * gw]
        width = 1
        while width < w:
            n = s.shape[0]
            s = s[width * batch:, :] + s[:n - width * batch, :]
            width *= 2
        wsum = s[s.shape[0] - rows:, :]
        cnt = jnp.minimum(pos1, w).astype(F32)
        pooled = wsum / cnt - u_pool[:, g * gw:(g + 1) * gw]
        pool_outs.append(_dot(pooled.astype(BF16), wpool_ref[g]))
    pool_out = jnp.concatenate(pool_outs, axis=1) * pscale_ref[...]

    for c in range(n_chunk):
        bu = _dot(u_ssm[:, c * ch_in:(c + 1) * ch_in].astype(BF16), bblk_ref[c])
        bur_ref[:, c * ch_st:(c + 1) * ch_st] = bu[:, :ch_st]
        bui_ref[:, c * ch_st:(c + 1) * ch_st] = bu[:, ch_st:]

    n_bt = batch // SUBLANES
    for q in range(n_state // SCAN_LANES):
        lanes = slice(q * SCAN_LANES, (q + 1) * SCAN_LANES)
        lr = jnp.broadcast_to(lre_ref[:, lanes], (SUBLANES, SCAN_LANES))
        li = jnp.broadcast_to(lim_ref[:, lanes], (SUBLANES, SCAN_LANES))

        def scan_tile(bt, carry, lanes=lanes, lr=lr, li=li):
            r0 = pl.multiple_of(bt * SUBLANES, SUBLANES)
            srows = pl.ds(r0, SUBLANES)

            def step(t, hc):
                hr, hi = hc
                trows = pl.ds(pl.multiple_of(t * batch + r0, SUBLANES), SUBLANES)
                nhr = lr * hr - li * hi + bur_ref[trows, lanes]
                nhi = lr * hi + li * hr + bui_ref[trows, lanes]
                bur_ref[trows, lanes] = nhr
                bui_ref[trows, lanes] = nhi
                return nhr, nhi

            hr, hi = lax.fori_loop(0, lt, step, (sre_ref[srows, lanes], sim_ref[srows, lanes]),
                                   unroll=min(lt, 4))
            sre_ref[srows, lanes] = hr
            sim_ref[srows, lanes] = hi
            return carry

        if n_bt == 1:
            scan_tile(0, 0)
        else:
            lax.fori_loop(0, n_bt, scan_tile, 0)

    ys = []
    for c in range(n_chunk):
        st = slice(c * ch_st, (c + 1) * ch_st)
        hcat = jnp.concatenate([bur_ref[:, st], bui_ref[:, st]], axis=1).astype(BF16)
        ys.append(_dot(hcat, cblk_ref[c]))
    y = jnp.concatenate(ys, axis=1) + dskip_ref[...] * u_ssm
    z = jax.nn.gelu(y)
    ssm_out = z * jax.nn.sigmoid(_dot(z.astype(BF16), wglu_ref[...]) + bglu_ref[...])

    mix = _dot(jnp.concatenate([pool_out, ssm_out], axis=1).astype(BF16), wout_ref[...])
    out_ref[...] = _layernorm(alpha * x + mix, ln1g_ref[...], ln1b_ref[...])

    @pl.when(i == nblk - 1)
    def _():
        opool_ref[...] = ext_ref[rows:rows + hist, :]
        osre_ref[...] = sre_ref[...]
        osim_ref[...] = sim_ref[...]

    if nblk > 1:
        @pl.when(i < nblk - 1)
        def _():
            ext_ref[0:hist, :] = ext_ref[rows:rows + hist, :]


def _mixer(layer, first, alpha, batch, start_pos, h, pool0, s0re, s0im, wts):
    t_rows, d_model = h.shape
    seq = t_rows // batch
    lt = max(1, min(seq, MIXER_ROWS // batch))
    assert seq % lt == 0 and batch % SUBLANES == 0
    nblk = seq // lt
    rows = lt * batch
    hist = POOL_BUF * batch
    assert nblk == 1 or rows >= hist
    d_pool = pool0.shape[1]
    n_state = s0re.shape[1]
    assert n_state % SCAN_LANES == 0

    def lw(arr):
        nd = arr.ndim - 1
        return pl.BlockSpec((None,) + arr.shape[1:], lambda i, _nd=nd: (layer,) + (0,) * _nd)

    def full(arr):
        nd = arr.ndim
        return pl.BlockSpec(arr.shape, lambda i, _nd=nd: (0,) * _nd)

    names = ["ln_in_g", "ln_in_b", "w_in", "w_pool", "pool_scale", "bblk", "lam_re", "lam_im",
             "cblk", "ssm_d", "w_glu", "b_glu", "w_out", "ln1_g", "ln1_b"]
    w_args = [wts[n] for n in names]
    w_specs = [full(wts[n]) if n.startswith("ln_in") else lw(wts[n]) for n in names]

    body = functools.partial(_mixer_body, nblk, batch, lt, start_pos, alpha, first)
    return pl.pallas_call(
        body,
        grid=(nblk,),
        in_specs=[pl.BlockSpec((rows, d_model), lambda i: (i, 0))] + w_specs
        + [full(pool0), full(s0re), full(s0im)],
        out_specs=(pl.BlockSpec((rows, d_model), lambda i: (i, 0)),
                   full(pool0), full(s0re), full(s0im)),
        out_shape=(jax.ShapeDtypeStruct(h.shape, F32), jax.ShapeDtypeStruct(pool0.shape, F32),
                   jax.ShapeDtypeStruct(s0re.shape, F32), jax.ShapeDtypeStruct(s0im.shape, F32)),
        scratch_shapes=[pltpu.VMEM((hist + rows, d_pool), F32),
                        pltpu.VMEM((rows, n_state), F32), pltpu.VMEM((rows, n_state), F32),
                        pltpu.VMEM((batch, n_state), F32), pltpu.VMEM((batch, n_state), F32)],
        compiler_params=pltpu.CompilerParams(dimension_semantics=("arbitrary",),
                                             vmem_limit_bytes=VMEM_LIMIT),
        name="mixer",
    )(h, *w_args, pool0, s0re, s0im)


def _route_transposed(s_t, sel_t):
    ng, pg = N_EXPERT_GROUPS, EXPERTS_PER_GROUP
    tn = s_t.shape[1]
    s = [s_t[j * ng:(j + 1) * ng, :] for j in range(pg)]
    v = [sel_t[j * ng:(j + 1) * ng, :] for j in range(pg)]
    neg = jnp.float32(-jnp.inf)

    m1 = v[0]
    for j in range(1, pg):
        m1 = jnp.maximum(m1, v[j])
    first_arg = jnp.full(m1.shape, pg, jnp.int32)
    for j in reversed(range(pg)):
        first_arg = jnp.where(v[j] == m1, j, first_arg)
    m2 = jnp.full(m1.shape, neg, F32)
    for j in range(pg):
        m2 = jnp.maximum(m2, jnp.where(first_arg == j, neg, v[j]))
    gscore = m1 + m2

    gio = lax.broadcasted_iota(jnp.int32, (ng, tn), 0)
    lower = [gio > g2 for g2 in range(ng)]
    lower_eq = [gio >= g2 for g2 in range(ng)]
    grank = jnp.zeros((ng, tn), jnp.int32)
    for g2 in range(ng):
        row = jnp.broadcast_to(gscore[g2:g2 + 1, :], (ng, tn))
        grank += ((row > gscore) | ((row == gscore) & lower[g2])).astype(jnp.int32)
    gmask = grank < TOPK_GROUPS

    mk = [jnp.where(gmask, v[j], neg) for j in range(pg)]
    rank = [jnp.zeros((ng, tn), jnp.int32) for _ in range(pg)]
    for j2 in range(pg):
        for g2 in range(ng):
            row = jnp.broadcast_to(mk[j2][g2:g2 + 1, :], (ng, tn))
            for j in range(pg):
                before = lower_eq[g2] if j2 < j else lower[g2]
                rank[j] += ((row > mk[j]) | ((row == mk[j]) & before)).astype(jnp.int32)

    selw = [jnp.where(rank[j] < TOP_K, s[j], 0.0) for j in range(pg)]
    tot = selw[0]
    for j in range(1, pg):
        tot = tot + selw[j]
    denom = jnp.sum(tot, axis=0, keepdims=True)
    gates = [selw[j] / denom * ROUTED_SCALE for j in range(pg)]
    pad = jnp.zeros((s_t.shape[0] - ng * pg, tn), F32)
    return jnp.concatenate(gates + [pad], axis=0)


def _ffn_body(nchunk, pairs, alpha,
              x_ref, p_ref, wr_ref, br_ref, wgu_ref, wd_ref, wsgu_ref, wsd_ref,
              wpg_ref, bpg_ref, wpp_ref, g_ref, b_ref,
              out_ref, xb_ref, gate_ref):
    c = pl.program_id(1)
    tb = x_ref.shape[0]

    @pl.when(c == 0)
    def _():
        x = x_ref[...]
        xh = x.astype(BF16)
        xb_ref[...] = xh
        xl = (x - xh.astype(F32)).astype(BF16)
        w = wr_ref[...]
        wh = w.astype(BF16)
        wl = (w - wh.astype(F32)).astype(BF16)
        logits = _dot(xh, wh) + (_dot(xl, wh) + _dot(xh, wl))
        scores = jax.nn.sigmoid(logits)
        sel = scores + br_ref[...]
        gate_ref[...] = _route_transposed(scores.T, sel.T).T
        out_ref[...] = jnp.zeros(out_ref.shape, F32)

    xb = xb_ref[...]
    gates = gate_ref[...]
    half = wgu_ref.shape[2] // 2
    acts = []
    for i in range(pairs):
        hgu = _dot(xb, wgu_ref[i])
        hg = hgu[:, :half]
        hu = hgu[:, half:]
        slot = (c * pairs + i) * 2
        g0 = jnp.take_along_axis(gates, jnp.full(gates.shape, slot, jnp.int32), axis=1)
        g1 = jnp.take_along_axis(gates, jnp.full(gates.shape, slot + 1, jnp.int32), axis=1)
        reps = half // 2 // LANES
        gexp = jnp.concatenate([g0] * reps + [g1] * reps, axis=1)
        acts.append((hg * jax.nn.sigmoid(hg) * hu * gexp).astype(BF16))
    out_ref[...] += _dot(jnp.concatenate(acts, axis=1), wd_ref[...])

    @pl.when(c == nchunk - 1)
    def _():
        x = x_ref[...]
        d_sh = wsd_ref.shape[0]
        hs = _dot(xb, wsgu_ref[...])
        hsg = hs[:, :d_sh]
        sh = (hsg * jax.nn.sigmoid(hsg) * hs[:, d_sh:]).astype(BF16)
        shared = _dot(sh, wsd_ref[...])
        pgate = jax.nn.sigmoid(_dot(xb, wpg_ref[...]) + bpg_ref[...])
        ple = pgate * _dot(p_ref[...].astype(BF16), wpp_ref[...])
        y = alpha * x + (out_ref[...] + shared) + ple
        out_ref[...] = _layernorm(y, g_ref[...], b_ref[...])


def _ffn(layer, alpha, h, p_all, wts):
    t_rows, d_model = h.shape
    tb = min(FFN_ROWS, t_rows)
    assert t_rows % tb == 0 and tb % LANES == 0
    n_pairs = wts["wgu"].shape[1]
    pairs = min(FFN_PAIRS, n_pairs)
    assert n_pairs % pairs == 0
    nchunk = n_pairs // pairs
    d_ple = p_all.shape[2]
    hid2 = wts["wgu"].shape[3] // 2

    def lw(arr):
        nd = arr.ndim - 1
        return pl.BlockSpec((None,) + arr.shape[1:], lambda i, c, _nd=nd: (layer,) + (0,) * _nd)

    body = functools.partial(_ffn_body, nchunk, pairs, alpha)
    return pl.pallas_call(
        body,
        grid=(t_rows // tb, nchunk),
        in_specs=[
            pl.BlockSpec((tb, d_model), lambda i, c: (i, 0)),
            pl.BlockSpec((None, tb, d_ple), lambda i, c: (layer, i, 0)),
            lw(wts["w_router"]), lw(wts["b_router"]),
            pl.BlockSpec((None, pairs, d_model, 2 * hid2), lambda i, c: (layer, c, 0, 0)),
            pl.BlockSpec((None, pairs * hid2, d_model), lambda i, c: (layer, c, 0)),
            lw(wts["ws_gu"]), lw(wts["ws_down"]), lw(wts["w_pg"]), lw(wts["b_pg"]), lw(wts["w_pp"]),
            lw(wts["ln2_g"]), lw(wts["ln2_b"]),
        ],
        out_specs=pl.BlockSpec((tb, d_model), lambda i, c: (i, 0)),
        out_shape=jax.ShapeDtypeStruct(h.shape, F32),
        scratch_shapes=[pltpu.VMEM((tb, d_model), BF16), pltpu.VMEM((tb, LANES), F32)],
        compiler_params=pltpu.CompilerParams(dimension_semantics=("parallel", "arbitrary"),
                                             vmem_limit_bytes=VMEM_LIMIT),
        name="ffn",
    )(h, p_all, wts["w_router"], wts["b_router"], wts["wgu"], wts["wd"], wts["ws_gu"],
      wts["ws_down"], wts["w_pg"], wts["b_pg"], wts["w_pp"], wts["ln2_g"], wts["ln2_b"])


def _block_diag(blocks, per_chunk):
    depth, groups, a, b = blocks.shape
    nch = groups // per_chunk
    eye = jnp.eye(per_chunk, dtype=blocks.dtype)
    x = blocks.reshape(depth, nch, per_chunk, a, 1, b) * eye[None, None, :, None, :, None]
    return x.reshape(depth, nch, per_chunk * a, per_chunk * b)


def _prepare_weights(ln_in_g, ln_in_b, w_in, w_pool, pool_scale, ssm_a_re, ssm_a_im, ssm_log_dt,
                     ssm_b_re, ssm_b_im, ssm_c_re, ssm_c_im, ssm_d, w_glu, b_glu, w_out, ln1_g, ln1_b,
                     w_router, b_router, w_gate, w_up, w_down, ws_gate, ws_up, ws_down,
                     w_pg, b_pg, w_pp, ln2_g, ln2_b):
    depth, d_model, _ = w_in.shape
    groups, n_st = ssm_a_re.shape[1:]
    row = lambda a: a.reshape(depth, 1, a.shape[-1])

    log_dt_b = jnp.broadcast_to(ssm_log_dt[..., None], (depth, groups, n_st)).reshape(depth * groups, n_st)
    bt_re = ssm_b_re.transpose(0, 1, 3, 2).reshape(depth * groups, SSM_H, n_st)
    bt_im = ssm_b_im.transpose(0, 1, 3, 2).reshape(depth * groups, SSM_H, n_st)
    lam_re, lam_im, bb_re, bb_im = _s5_prep(
        ssm_a_re.reshape(depth * groups, n_st), ssm_a_im.reshape(depth * groups, n_st), log_dt_b, bt_re, bt_im)
    per_chunk = LANES // SSM_H
    bb_re = bb_re.reshape(depth, groups, SSM_H, n_st)
    bb_im = bb_im.reshape(depth, groups, SSM_H, n_st)
    bblk = jnp.concatenate([_block_diag(bb_re, per_chunk), _block_diag(bb_im, per_chunk)], axis=-1).astype(BF16)
    ct_re = ssm_c_re.transpose(0, 1, 3, 2)
    ct_im = ssm_c_im.transpose(0, 1, 3, 2)
    cblk = jnp.concatenate([_block_diag(ct_re, per_chunk), -_block_diag(ct_im, per_chunk)], axis=-2).astype(BF16)

    n_exp, _, d_exp = w_gate.shape[1:]
    ng, pg = N_EXPERT_GROUPS, EXPERTS_PER_GROUP

    def pair_cols(w):
        w = w.reshape(depth, ng // 2, 2, pg, d_model, d_exp).transpose(0, 3, 1, 4, 2, 5)
        return w.reshape(depth, n_exp // 2, d_model, 2 * d_exp)

    wgu = jnp.concatenate([pair_cols(w_gate), pair_cols(w_up)], axis=-1).astype(BF16)
    wd = w_down.reshape(depth, ng, pg, d_exp, d_model).transpose(0, 2, 1, 3, 4)
    wd = wd.reshape(depth, n_exp * d_exp, d_model).astype(BF16)

    def slot_cols(a):
        lead = a.shape[:-1]
        a = a.reshape(lead + (ng, pg)).swapaxes(-1, -2).reshape(lead + (n_exp,))
        return jnp.pad(a, [(0, 0)] * len(lead) + [(0, LANES - n_exp)])

    return {
        "ln_in_g": ln_in_g.reshape(1, d_model), "ln_in_b": ln_in_b.reshape(1, d_model),
        "w_in": w_in.astype(BF16), "w_pool": w_pool.astype(BF16), "pool_scale": row(pool_scale),
        "bblk": bblk, "lam_re": lam_re.reshape(depth, 1, groups * n_st),
        "lam_im": lam_im.reshape(depth, 1, groups * n_st), "cblk": cblk,
        "ssm_d": row(ssm_d), "w_glu": w_glu.astype(BF16), "b_glu": row(b_glu),
        "w_out": w_out.astype(BF16), "ln1_g": row(ln1_g), "ln1_b": row(ln1_b),
        "w_router": slot_cols(w_router), "b_router": row(slot_cols(b_router)),
        "wgu": wgu, "wd": wd,
        "ws_gu": jnp.concatenate([ws_gate, ws_up], axis=-1).astype(BF16), "ws_down": ws_down.astype(BF16),
        "w_pg": w_pg.astype(BF16), "b_pg": row(b_pg), "w_pp": w_pp.astype(BF16),
        "ln2_g": row(ln2_g), "ln2_b": row(ln2_b),
    }


def _time_major(x):
    b, s, d = x.shape
    return x.transpose(1, 0, 2).reshape(s * b, d)


def _batch_major(x, batch):
    t, d = x.shape
    return x.reshape(t // batch, batch, d).transpose(1, 0, 2)


def kernel(x_prompt, x_sample, state_pool, state_ssm_re, state_ssm_im, p_prompt, p_sample, ln_in_g, ln_in_b, w_in, w_pool, pool_scale, ssm_a_re, ssm_a_im, ssm_log_dt, ssm_b_re, ssm_b_im, ssm_c_re, ssm_c_im, ssm_d, w_glu, b_glu, w_out, ln1_g, ln1_b, w_router, b_router, w_gate, w_up, w_down, ws_gate, ws_up, ws_down, w_pg, b_pg, w_pp, ln2_g, ln2_b):
    depth = w_in.shape[0]
    alpha = (2.0 * depth) ** 0.25
    nb_p, nb_s = x_prompt.shape[0], x_sample.shape[0]
    d_pool = state_pool.shape[-1]
    groups, n_st = state_ssm_re.shape[-2:]
    n_state = groups * n_st

    wts = _prepare_weights(ln_in_g, ln_in_b, w_in, w_pool, pool_scale, ssm_a_re, ssm_a_im, ssm_log_dt,
                           ssm_b_re, ssm_b_im, ssm_c_re, ssm_c_im, ssm_d, w_glu, b_glu, w_out, ln1_g, ln1_b,
                           w_router, b_router, w_gate, w_up, w_down, ws_gate, ws_up, ws_down,
                           w_pg, b_pg, w_pp, ln2_g, ln2_b)

    hp = _time_major(x_prompt)
    hs = _time_major(x_sample)
    pp = p_prompt.transpose(0, 2, 1, 3).reshape(depth, hp.shape[0], p_prompt.shape[-1])
    ps = p_sample.transpose(0, 2, 1, 3).reshape(depth, hs.shape[0], p_sample.shape[-1])
    pool0_p = jnp.zeros((POOL_BUF * nb_p, d_pool), F32)
    s0_p = jnp.zeros((nb_p, n_state), F32)
    pool0_s = state_pool.transpose(0, 2, 1, 3).reshape(depth, POOL_BUF * nb_s, d_pool)
    s0re_s = state_ssm_re.reshape(depth, nb_s, n_state)
    s0im_s = state_ssm_im.reshape(depth, nb_s, n_state)

    outs = {k: [] for k in ("pool_p", "re_p", "im_p", "pool_s", "re_s", "im_s")}
    for l in range(depth):
        hp, bp, rp, ip = _mixer(l, l == 0, alpha, nb_p, 0, hp, pool0_p, s0_p, s0_p, wts)
        hp = _ffn(l, alpha, hp, pp, wts)
        hs, bs, rs, is_ = _mixer(l, l == 0, alpha, nb_s, PAST_LEN, hs, pool0_s[l], s0re_s[l], s0im_s[l], wts)
        hs = _ffn(l, alpha, hs, ps, wts)
        outs["pool_p"].append(_batch_major(bp, nb_p))
        outs["re_p"].append(rp.reshape(nb_p, groups, n_st))
        outs["im_p"].append(ip.reshape(nb_p, groups, n_st))
        outs["pool_s"].append(_batch_major(bs, nb_s))
        outs["re_s"].append(rs.reshape(nb_s, groups, n_st))
        outs["im_s"].append(is_.reshape(nb_s, groups, n_st))

    return (_batch_major(hp, nb_p), _batch_major(hs, nb_s),
            jnp.stack(outs["pool_p"]), jnp.stack(outs["re_p"]), jnp.stack(outs["im_p"]),
            jnp.stack(outs["pool_s"]), jnp.stack(outs["re_s"]), jnp.stack(outs["im_s"]))
```

```python
import functools

import jax
import jax.numpy as jnp
from jax import lax
from jax.experimental import pallas as pl
from jax.experimental.pallas import tpu as pltpu

F32 = jnp.float32
BF16 = jnp.bfloat16

POOL_WINDOWS = (2, 4, 8, 16)
POOL_BUF = max(POOL_WINDOWS) - 1
SSM_H = 16
SSM_STATE = 64
N_EXPERT_GROUPS = 8
EXPERTS_PER_GROUP = 8
TOPK_GROUPS = 4
TOP_K = 8
ROUTED_SCALE = 2.5
PAST_LEN = 16384
LN_EPS = 1e-5

LANES = 128
SUBLANES = 8
SCAN_LANES = 512
MIXER_ROWS = 512
FFN_ROWS = 1024
FFN_PAIRS = 4
VMEM_LIMIT = 56 * 1024 * 1024


def _dot(a, b):
    return jnp.dot(a, b, preferred_element_type=F32)


def _layernorm(x, g, b):
    mu = jnp.mean(x, axis=-1, keepdims=True)
    xc = x - mu
    var = jnp.mean(xc * xc, axis=-1, keepdims=True)
    return xc * lax.rsqrt(var + LN_EPS) * g + b


def _s5_prep_body(are_ref, aim_ref, ldt_ref, bre_ref, bim_ref,
                  lre_ref, lim_ref, bbre_ref, bbim_ref):
    a_re = are_ref[...]
    a_im = aim_ref[...]
    dt = jnp.exp(ldt_ref[...])
    mag = jnp.exp(a_re * dt)
    lam_re = mag * jnp.cos(a_im * dt)
    lam_im = mag * jnp.sin(a_im * dt)
    den = a_re * a_re + a_im * a_im
    nr = lam_re - 1.0
    ni = lam_im
    f_re = ((nr * a_re + ni * a_im) / den)[:, None, :]
    f_im = ((ni * a_re - nr * a_im) / den)[:, None, :]
    b_re = bre_ref[...]
    b_im = bim_ref[...]
    lre_ref[...] = lam_re
    lim_ref[...] = lam_im
    bbre_ref[...] = f_re * b_re - f_im * b_im
    bbim_ref[...] = f_re * b_im + f_im * b_re


def _s5_prep(a_re, a_im, log_dt_b, bt_re, bt_im):
    n, p = a_re.shape
    return pl.pallas_call(
        _s5_prep_body,
        out_shape=(jax.ShapeDtypeStruct((n, p), F32), jax.ShapeDtypeStruct((n, p), F32),
                   jax.ShapeDtypeStruct(bt_re.shape, F32), jax.ShapeDtypeStruct(bt_im.shape, F32)),
        name="s5_prep",
    )(a_re, a_im, log_dt_b, bt_re, bt_im)


def _mixer_body(nblk, batch, lt, start_pos, alpha, first,
                h_ref, lng_ref, lnb_ref, win_ref, wpool_ref, pscale_ref, bblk_ref,
                lre_ref, lim_ref, cblk_ref, dskip_ref, wglu_ref, bglu_ref, wout_ref,
                ln1g_ref, ln1b_ref, pool0_ref, s0re_ref, s0im_ref,
                out_ref, opool_ref, osre_ref, osim_ref,
                ext_ref, bur_ref, bui_ref, sre_ref, sim_ref):
    i = pl.program_id(0)
    rows = lt * batch
    hist = POOL_BUF * batch
    d_pool = ext_ref.shape[1]
    n_state = bur_ref.shape[1]
    gw = d_pool // len(POOL_WINDOWS)
    n_chunk = bblk_ref.shape[0]
    ch_in = bblk_ref.shape[1]
    ch_st = bblk_ref.shape[2] // 2

    @pl.when(i == 0)
    def _():
        ext_ref[0:hist, :] = pool0_ref[...]
        sre_ref[...] = s0re_ref[...]
        sim_ref[...] = s0im_ref[...]

    x = h_ref[...]
    if first:
        x = _layernorm(x, lng_ref[...], lnb_ref[...])
    u = _dot(x.astype(BF16), win_ref[...])
    u_pool = u[:, :d_pool]
    u_ssm = u[:, d_pool:]
    ext_ref[hist:hist + rows, :] = u_pool

    t_loc = lax.broadcasted_iota(jnp.int32, (lt, batch, gw), 0).reshape(rows, gw)
    pos1 = t_loc + (start_pos + 1) + i * lt
    pool_outs = []
    for g, w in enumerate(POOL_WINDOWS):
        s = ext_ref[:, g * gw:(g + 1) * gw]
        width = 1
        while width < w:
            n = s.shape[0]
            s = s[width * batch:, :] + s[:n - width * batch, :]
            width *= 2
        wsum = s[s.shape[0] - rows:, :]
        cnt = jnp.minimum(pos1, w).astype(F32)
        pooled = wsum / cnt - u_pool[:, g * gw:(g + 1) * gw]
        pool_outs.append(_dot(pooled.astype(BF16), wpool_ref[g]))
    pool_out = jnp.concatenate(pool_outs, axis=1) * pscale_ref[...]

    for c in range(n_chunk):
        bu = _dot(u_ssm[:, c * ch_in:(c + 1) * ch_in].astype(BF16), bblk_ref[c])
        bur_ref[:, c * ch_st:(c + 1) * ch_st] = bu[:, :ch_st]
        bui_ref[:, c * ch_st:(c + 1) * ch_st] = bu[:, ch_st:]

    n_bt = batch // SUBLANES
    for q in range(n_state // SCAN_LANES):
        lanes = slice(q * SCAN_LANES, (q + 1) * SCAN_LANES)
        lr = jnp.broadcast_to(lre_ref[:, lanes], (SUBLANES, SCAN_LANES))
        li = jnp.broadcast_to(lim_ref[:, lanes], (SUBLANES, SCAN_LANES))

        def scan_tile(bt, carry, lanes=lanes, lr=lr, li=li):
            r0 = pl.multiple_of(bt * SUBLANES, SUBLANES)
            srows = pl.ds(r0, SUBLANES)

            def step(t, hc):
                hr, hi = hc
                trows = pl.ds(pl.multiple_of(t * batch + r0, SUBLANES), SUBLANES)
                nhr = lr * hr - li * hi + bur_ref[trows, lanes]
                nhi = lr * hi + li * hr + bui_ref[trows, lanes]
                bur_ref[trows, lanes] = nhr
                bui_ref[trows, lanes] = nhi
                return nhr, nhi

            hr, hi = lax.fori_loop(0, lt, step, (sre_ref[srows, lanes], sim_ref[srows, lanes]),
                                   unroll=min(lt, 4))
            sre_ref[srows, lanes] = hr
            sim_ref[srows, lanes] = hi
            return carry

        if n_bt == 1:
            scan_tile(0, 0)
        else:
            lax.fori_loop(0, n_bt, scan_tile, 0)

    ys = []
    for c in range(n_chunk):
        st = slice(c * ch_st, (c + 1) * ch_st)
        hcat = jnp.concatenate([bur_ref[:, st], bui_ref[:, st]], axis=1).astype(BF16)
        ys.append(_dot(hcat, cblk_ref[c]))
    y = jnp.concatenate(ys, axis=1) + dskip_ref[...] * u_ssm
    z = jax.nn.gelu(y)
    ssm_out = z * jax.nn.sigmoid(_dot(z.astype(BF16), wglu_ref[...]) + bglu_ref[...])

    mix = _dot(jnp.concatenate([pool_out, ssm_out], axis=1).astype(BF16), wout_ref[...])
    out_ref[...] = _layernorm(alpha * x + mix, ln1g_ref[...], ln1b_ref[...])

    @pl.when(i == nblk - 1)
    def _():
        opool_ref[...] = ext_ref[rows:rows + hist, :]
        osre_ref[...] = sre_ref[...]
        osim_ref[...] = sim_ref[...]

    if nblk > 1:
        @pl.when(i < nblk - 1)
        def _():
            ext_ref[0:hist, :] = ext_ref[rows:rows + hist, :]


def _mixer(layer, first, alpha, batch, start_pos, h, pool0, s0re, s0im, wts):
    t_rows, d_model = h.shape
    seq = t_rows // batch
    lt = max(1, min(seq, MIXER_ROWS // batch))
    assert seq % lt == 0 and batch % SUBLANES == 0
    nblk = seq // lt
    rows = lt * batch
    hist = POOL_BUF * batch
    assert nblk == 1 or rows >= hist
    d_pool = pool0.shape[1]
    n_state = s0re.shape[1]
    assert n_state % SCAN_LANES == 0

    def lw(arr):
        nd = arr.ndim - 1
        return pl.BlockSpec((None,) + arr.shape[1:], lambda i, _nd=nd: (layer,) + (0,) * _nd)

    def full(arr):
        nd = arr.ndim
        return pl.BlockSpec(arr.shape, lambda i, _nd=nd: (0,) * _nd)

    names = ["ln_in_g", "ln_in_b", "w_in", "w_pool", "pool_scale", "bblk", "lam_re", "lam_im",
             "cblk", "ssm_d", "w_glu", "b_glu", "w_out", "ln1_g", "ln1_b"]
    w_args = [wts[n] for n in names]
    w_specs = [full(wts[n]) if n.startswith("ln_in") else lw(wts[n]) for n in names]

    body = functools.partial(_mixer_body, nblk, batch, lt, start_pos, alpha, first)
    return pl.pallas_call(
        body,
        grid=(nblk,),
        in_specs=[pl.BlockSpec((rows, d_model), lambda i: (i, 0))] + w_specs
        + [full(pool0), full(s0re), full(s0im)],
        out_specs=(pl.BlockSpec((rows, d_model), lambda i: (i, 0)),
                   full(pool0), full(s0re), full(s0im)),
        out_shape=(jax.ShapeDtypeStruct(h.shape, F32), jax.ShapeDtypeStruct(pool0.shape, F32),
                   jax.ShapeDtypeStruct(s0re.shape, F32), jax.ShapeDtypeStruct(s0im.shape, F32)),
        scratch_shapes=[pltpu.VMEM((hist + rows, d_pool), F32),
                        pltpu.VMEM((rows, n_state), F32), pltpu.VMEM((rows, n_state), F32),
                        pltpu.VMEM((batch, n_state), F32), pltpu.VMEM((batch, n_state), F32)],
        compiler_params=pltpu.CompilerParams(dimension_semantics=("arbitrary",),
                                             vmem_limit_bytes=VMEM_LIMIT),
        name="mixer",
    )(h, *w_args, pool0, s0re, s0im)


def _route_transposed(s_t, sel_t):
    ng, pg = N_EXPERT_GROUPS, EXPERTS_PER_GROUP
    tn = s_t.shape[1]
    s = [s_t[j * ng:(j + 1) * ng, :] for j in range(pg)]
    v = [sel_t[j * ng:(j + 1) * ng, :] for j in range(pg)]
    neg = jnp.float32(-jnp.inf)

    m1 = v[0]
    for j in range(1, pg):
        m1 = jnp.maximum(m1, v[j])
    first_arg = jnp.full(m1.shape, pg, jnp.int32)
    for j in reversed(range(pg)):
        first_arg = jnp.where(v[j] == m1, j, first_arg)
    m2 = jnp.full(m1.shape, neg, F32)
    for j in range(pg):
        m2 = jnp.maximum(m2, jnp.where(first_arg == j, neg, v[j]))
    gscore = m1 + m2

    gio = lax.broadcasted_iota(jnp.int32, (ng, tn), 0)
    lower = [gio > g2 for g2 in range(ng)]
    grank = jnp.zeros((ng, tn), jnp.int32)
    for g2 in range(ng):
        row = jnp.broadcast_to(gscore[g2:g2 + 1, :], (ng, tn))
        grank += ((row > gscore) | ((row == gscore) & lower[g2])).astype(jnp.int32)
    gmask = grank < TOPK_GROUPS

    mk = [jnp.where(gmask, v[j], neg) for j in range(pg)]
    eidx = [gio * pg + j for j in range(pg)]
    taken = [jnp.zeros((ng, tn), jnp.bool_) for _ in range(pg)]
    for _ in range(TOP_K):
        m = mk[0]
        for j in range(1, pg):
            m = jnp.maximum(m, mk[j])
        m = jnp.broadcast_to(jnp.max(m, axis=0, keepdims=True), (ng, tn))
        key = [jnp.where(mk[j] == m, eidx[j], ng * pg) for j in range(pg)]
        kmin = key[0]
        for j in range(1, pg):
            kmin = jnp.minimum(kmin, key[j])
        kmin = jnp.broadcast_to(jnp.min(kmin, axis=0, keepdims=True), (ng, tn))
        for j in range(pg):
            pick = key[j] == kmin
            taken[j] = taken[j] | pick
            mk[j] = jnp.where(pick, neg, mk[j])

    selw = [jnp.where(taken[j], s[j], 0.0) for j in range(pg)]
    tot = selw[0]
    for j in range(1, pg):
        tot = tot + selw[j]
    denom = jnp.sum(tot, axis=0, keepdims=True)
    gates = [selw[j] / denom * ROUTED_SCALE for j in range(pg)]
    pad = jnp.zeros((s_t.shape[0] - ng * pg, tn), F32)
    return jnp.concatenate(gates + [pad], axis=0)


def _ffn_body(nchunk, pairs, alpha,
              x_ref, p_ref, wr_ref, br_ref, wg_ref, wu_ref, wd_ref, wsgu_ref, wsd_ref,
              wpg_ref, bpg_ref, wpp_ref, g_ref, b_ref,
              out_ref, xb_ref, gate_ref):
    c = pl.program_id(1)
    pg = EXPERTS_PER_GROUP

    @pl.when(c == 0)
    def _():
        x = x_ref[...]
        xh = x.astype(BF16)
        xb_ref[...] = xh
        xl = (x - xh.astype(F32)).astype(BF16)
        w = wr_ref[...]
        wh = w.astype(BF16)
        wl = (w - wh.astype(F32)).astype(BF16)
        logits = _dot(xh, wh) + (_dot(xl, wh) + _dot(xh, wl))
        scores = jax.nn.sigmoid(logits)
        sel = scores + br_ref[...]
        gate_ref[...] = _route_transposed(scores.T, sel.T).T

        d_sh = wsd_ref.shape[0]
        hs = _dot(xh, wsgu_ref[...])
        hsg = hs[:, :d_sh]
        sh = (hsg * jax.nn.sigmoid(hsg) * hs[:, d_sh:]).astype(BF16)
        shared = _dot(sh, wsd_ref[...])
        pgate = jax.nn.sigmoid(_dot(xh, wpg_ref[...]) + bpg_ref[...])
        ple = pgate * _dot(p_ref[...].astype(BF16), wpp_ref[...])
        out_ref[...] = alpha * x + shared + ple

    xb = xb_ref[...]
    gates = gate_ref[...]
    acts = []
    for i in range(pairs):
        wcat = jnp.concatenate([wg_ref[2 * i], wg_ref[2 * i + 1], wu_ref[2 * i], wu_ref[2 * i + 1]], axis=1)
        hgu = _dot(xb, wcat)
        half = hgu.shape[1] // 2
        hg = hgu[:, :half]
        hu = hgu[:, half:]
        e0 = (c * pairs + i) * 2
        lane0 = (e0 % pg) * N_EXPERT_GROUPS + e0 // pg
        g0 = jnp.take_along_axis(gates, jnp.full(gates.shape, lane0, jnp.int32), axis=1)
        g1 = jnp.take_along_axis(gates, jnp.full(gates.shape, lane0 + N_EXPERT_GROUPS, jnp.int32), axis=1)
        reps = half // 2 // LANES
        gexp = jnp.concatenate([g0] * reps + [g1] * reps, axis=1)
        acts.append((hg * jax.nn.sigmoid(hg) * hu * gexp).astype(BF16))
    wd = wd_ref[...]
    routed = _dot(jnp.concatenate(acts, axis=1), wd.reshape(wd.shape[0] * wd.shape[1], wd.shape[2]))

    @pl.when(c < nchunk - 1)
    def _():
        out_ref[...] += routed

    @pl.when(c == nchunk - 1)
    def _():
        out_ref[...] = _layernorm(out_ref[...] + routed, g_ref[...], b_ref[...])


def _ffn(layer, alpha, h, p_all, wts):
    t_rows, d_model = h.shape
    tb = min(FFN_ROWS, t_rows)
    assert t_rows % tb == 0 and tb % LANES == 0
    n_exp, _, d_exp = wts["w_gate"].shape[1:]
    pairs = min(FFN_PAIRS, n_exp // 2)
    assert n_exp % (2 * pairs) == 0 and EXPERTS_PER_GROUP % 2 == 0
    nchunk = n_exp // (2 * pairs)
    d_ple = p_all.shape[2]

    def lw(arr):
        nd = arr.ndim - 1
        return pl.BlockSpec((None,) + arr.shape[1:], lambda i, c, _nd=nd: (layer,) + (0,) * _nd)

    body = functools.partial(_ffn_body, nchunk, pairs, alpha)
    return pl.pallas_call(
        body,
        grid=(t_rows // tb, nchunk),
        in_specs=[
            pl.BlockSpec((tb, d_model), lambda i, c: (i, 0)),
            pl.BlockSpec((None, tb, d_ple), lambda i, c: (layer, i, 0)),
            lw(wts["w_router"]), lw(wts["b_router"]),
            pl.BlockSpec((None, 2 * pairs, d_model, d_exp), lambda i, c: (layer, c, 0, 0)),
            pl.BlockSpec((None, 2 * pairs, d_model, d_exp), lambda i, c: (layer, c, 0, 0)),
            pl.BlockSpec((None, 2 * pairs, d_exp, d_model), lambda i, c: (layer, c, 0, 0)),
            lw(wts["ws_gu"]), lw(wts["ws_down"]), lw(wts["w_pg"]), lw(wts["b_pg"]), lw(wts["w_pp"]),
            lw(wts["ln2_g"]), lw(wts["ln2_b"]),
        ],
        out_specs=pl.BlockSpec((tb, d_model), lambda i, c: (i, 0)),
        out_shape=jax.ShapeDtypeStruct(h.shape, F32),
        scratch_shapes=[pltpu.VMEM((tb, d_model), BF16), pltpu.VMEM((tb, LANES), F32)],
        compiler_params=pltpu.CompilerParams(dimension_semantics=("parallel", "arbitrary"),
                                             vmem_limit_bytes=VMEM_LIMIT),
        name="ffn",
    )(h, p_all, wts["w_router"], wts["b_router"], wts["w_gate"], wts["w_up"], wts["w_down"], wts["ws_gu"],
      wts["ws_down"], wts["w_pg"], wts["b_pg"], wts["w_pp"], wts["ln2_g"], wts["ln2_b"])


def _block_diag(blocks, per_chunk):
    depth, groups, a, b = blocks.shape
    nch = groups // per_chunk
    eye = jnp.eye(per_chunk, dtype=blocks.dtype)
    x = blocks.reshape(depth, nch, per_chunk, a, 1, b) * eye[None, None, :, None, :, None]
    return x.reshape(depth, nch, per_chunk * a, per_chunk * b)


def _prepare_weights(ln_in_g, ln_in_b, w_in, w_pool, pool_scale, ssm_a_re, ssm_a_im, ssm_log_dt,
                     ssm_b_re, ssm_b_im, ssm_c_re, ssm_c_im, ssm_d, w_glu, b_glu, w_out, ln1_g, ln1_b,
                     w_router, b_router, w_gate, w_up, w_down, ws_gate, ws_up, ws_down,
                     w_pg, b_pg, w_pp, ln2_g, ln2_b):
    depth, d_model, _ = w_in.shape
    groups, n_st = ssm_a_re.shape[1:]
    row = lambda a: a.reshape(depth, 1, a.shape[-1])

    log_dt_b = jnp.broadcast_to(ssm_log_dt[..., None], (depth, groups, n_st)).reshape(depth * groups, n_st)
    bt_re = ssm_b_re.transpose(0, 1, 3, 2).reshape(depth * groups, SSM_H, n_st)
    bt_im = ssm_b_im.transpose(0, 1, 3, 2).reshape(depth * groups, SSM_H, n_st)
    lam_re, lam_im, bb_re, bb_im = _s5_prep(
        ssm_a_re.reshape(depth * groups, n_st), ssm_a_im.reshape(depth * groups, n_st), log_dt_b, bt_re, bt_im)
    per_chunk = LANES // SSM_H
    bb_re = bb_re.reshape(depth, groups, SSM_H, n_st)
    bb_im = bb_im.reshape(depth, groups, SSM_H, n_st)
    bblk = jnp.concatenate([_block_diag(bb_re, per_chunk), _block_diag(bb_im, per_chunk)], axis=-1).astype(BF16)
    ct_re = ssm_c_re.transpose(0, 1, 3, 2)
    ct_im = ssm_c_im.transpose(0, 1, 3, 2)
    cblk = jnp.concatenate([_block_diag(ct_re, per_chunk), -_block_diag(ct_im, per_chunk)], axis=-2).astype(BF16)

    n_exp = w_gate.shape[1]
    ng, pg = N_EXPERT_GROUPS, EXPERTS_PER_GROUP

    def slot_cols(a):
        lead = a.shape[:-1]
        a = a.reshape(lead + (ng, pg)).swapaxes(-1, -2).reshape(lead + (n_exp,))
        return jnp.pad(a, [(0, 0)] * len(lead) + [(0, LANES - n_exp)])

    return {
        "ln_in_g": ln_in_g.reshape(1, d_model), "ln_in_b": ln_in_b.reshape(1, d_model),
        "w_in": w_in.astype(BF16), "w_pool": w_pool.astype(BF16), "pool_scale": row(pool_scale),
        "bblk": bblk, "lam_re": lam_re.reshape(depth, 1, groups * n_st),
        "lam_im": lam_im.reshape(depth, 1, groups * n_st), "cblk": cblk,
        "ssm_d": row(ssm_d), "w_glu": w_glu.astype(BF16), "b_glu": row(b_glu),
        "w_out": w_out.astype(BF16), "ln1_g": row(ln1_g), "ln1_b": row(ln1_b),
        "w_router": slot_cols(w_router), "b_router": row(slot_cols(b_router)),
        "w_gate": w_gate.astype(BF16), "w_up": w_up.astype(BF16), "w_down": w_down.astype(BF16),
        "ws_gu": jnp.concatenate([ws_gate, ws_up], axis=-1).astype(BF16), "ws_down": ws_down.astype(BF16),
        "w_pg": w_pg.astype(BF16), "b_pg": row(b_pg), "w_pp": w_pp.astype(BF16),
        "ln2_g": row(ln2_g), "ln2_b": row(ln2_b),
    }


def _time_major(x):
    b, s, d = x.shape
    return x.transpose(1, 0, 2).reshape(s * b, d)


def _batch_major(x, batch):
    t, d = x.shape
    return x.reshape(t // batch, batch, d).transpose(1, 0, 2)


def kernel(x_prompt, x_sample, state_pool, state_ssm_re, state_ssm_im, p_prompt, p_sample, ln_in_g, ln_in_b, w_in, w_pool, pool_scale, ssm_a_re, ssm_a_im, ssm_log_dt, ssm_b_re, ssm_b_im, ssm_c_re, ssm_c_im, ssm_d, w_glu, b_glu, w_out, ln1_g, ln1_b, w_router, b_router, w_gate, w_up, w_down, ws_gate, ws_up, ws_down, w_pg, b_pg, w_pp, ln2_g, ln2_b):
    depth = w_in.shape[0]
    alpha = (2.0 * depth) ** 0.25
    nb_p, nb_s = x_prompt.shape[0], x_sample.shape[0]
    d_pool = state_pool.shape[-1]
    groups, n_st = state_ssm_re.shape[-2:]
    n_state = groups * n_st

    wts = _prepare_weights(ln_in_g, ln_in_b, w_in, w_pool, pool_scale, ssm_a_re, ssm_a_im, ssm_log_dt,
                           ssm_b_re, ssm_b_im, ssm_c_re, ssm_c_im, ssm_d, w_glu, b_glu, w_out, ln1_g, ln1_b,
                           w_router, b_router, w_gate, w_up, w_down, ws_gate, ws_up, ws_down,
                           w_pg, b_pg, w_pp, ln2_g, ln2_b)

    hp = _time_major(x_prompt)
    hs = _time_major(x_sample)
    pp = p_prompt.transpose(0, 2, 1, 3).reshape(depth, hp.shape[0], p_prompt.shape[-1])
    ps = p_sample.transpose(0, 2, 1, 3).reshape(depth, hs.shape[0], p_sample.shape[-1])
    pool0_p = jnp.zeros((POOL_BUF * nb_p, d_pool), F32)
    s0_p = jnp.zeros((nb_p, n_state), F32)
    pool0_s = state_pool.transpose(0, 2, 1, 3).reshape(depth, POOL_BUF * nb_s, d_pool)
    s0re_s = state_ssm_re.reshape(depth, nb_s, n_state)
    s0im_s = state_ssm_im.reshape(depth, nb_s, n_state)

    outs = {k: [] for k in ("pool_p", "re_p", "im_p", "pool_s", "re_s", "im_s")}
    for l in range(depth):
        hp, bp, rp, ip = _mixer(l, l == 0, alpha, nb_p, 0, hp, pool0_p, s0_p, s0_p, wts)
        hp = _ffn(l, alpha, hp, pp, wts)
        hs, bs, rs, is_ = _mixer(l, l == 0, alpha, nb_s, PAST_LEN, hs, pool0_s[l], s0re_s[l], s0im_s[l], wts)
        hs = _ffn(l, alpha, hs, ps, wts)
        outs["pool_p"].append(_batch_major(bp, nb_p))
        outs["re_p"].append(rp.reshape(nb_p, groups, n_st))
        outs["im_p"].append(ip.reshape(nb_p, groups, n_st))
        outs["pool_s"].append(_batch_major(bs, nb_s))
        outs["re_s"].append(rs.reshape(nb_s, groups, n_st))
        outs["im_s"].append(is_.reshape(nb_s, groups, n_st))

    return (_batch_major(hp, nb_p), _batch_major(hs, nb_s),
            jnp.stack(outs["pool_p"]), jnp.stack(outs["re_p"]), jnp.stack(outs["im_p"]),
            jnp.stack(outs["pool_s"]), jnp.stack(outs["re_s"]), jnp.stack(outs["im_s"]))
```

```python
import functools

import jax
import jax.numpy as jnp
from jax import lax
from jax.experimental import pallas as pl
from jax.experimental.pallas import tpu as pltpu

F32 = jnp.float32
BF16 = jnp.bfloat16

POOL_WINDOWS = (2, 4, 8, 16)
POOL_BUF = max(POOL_WINDOWS) - 1
SSM_H = 16
SSM_STATE = 64
N_EXPERT_GROUPS = 8
EXPERTS_PER_GROUP = 8
TOPK_GROUPS = 4
TOP_K = 8
ROUTED_SCALE = 2.5
PAST_LEN = 16384
LN_EPS = 1e-5

LANES = 128
SUBLANES = 8
SCAN_LANES = 512
MIXER_ROWS = 1024
FFN_ROWS = 1024
FFN_PAIRS = 4
VMEM_LIMIT = 56 * 1024 * 1024


def _dot(a, b):
    return jnp.dot(a, b, preferred_element_type=F32)


def _layernorm(x, g, b):
    mu = jnp.mean(x, axis=-1, keepdims=True)
    xc = x - mu
    var = jnp.mean(xc * xc, axis=-1, keepdims=True)
    return xc * lax.rsqrt(var + LN_EPS) * g + b


def _s5_prep_body(are_ref, aim_ref, ldt_ref, bre_ref, bim_ref,
                  lre_ref, lim_ref, bbre_ref, bbim_ref):
    a_re = are_ref[...]
    a_im = aim_ref[...]
    dt = jnp.exp(ldt_ref[...])
    mag = jnp.exp(a_re * dt)
    lam_re = mag * jnp.cos(a_im * dt)
    lam_im = mag * jnp.sin(a_im * dt)
    den = a_re * a_re + a_im * a_im
    nr = lam_re - 1.0
    ni = lam_im
    f_re = ((nr * a_re + ni * a_im) / den)[:, None, :]
    f_im = ((ni * a_re - nr * a_im) / den)[:, None, :]
    b_re = bre_ref[...]
    b_im = bim_ref[...]
    lre_ref[...] = lam_re
    lim_ref[...] = lam_im
    bbre_ref[...] = f_re * b_re - f_im * b_im
    bbim_ref[...] = f_re * b_im + f_im * b_re


def _s5_prep(a_re, a_im, log_dt_b, bt_re, bt_im):
    n, p = a_re.shape
    return pl.pallas_call(
        _s5_prep_body,
        out_shape=(jax.ShapeDtypeStruct((n, p), F32), jax.ShapeDtypeStruct((n, p), F32),
                   jax.ShapeDtypeStruct(bt_re.shape, F32), jax.ShapeDtypeStruct(bt_im.shape, F32)),
        name="s5_prep",
    )(a_re, a_im, log_dt_b, bt_re, bt_im)


def _mixer_body(nblk, batch, lt, start_pos, alpha, first,
                h_ref, lng_ref, lnb_ref, win_ref, wpool_ref, pscale_ref, bblk_ref,
                lre_ref, lim_ref, cblk_ref, dskip_ref, wglu_ref, bglu_ref, wout_ref,
                ln1g_ref, ln1b_ref, pool0_ref, s0re_ref, s0im_ref,
                out_ref, opool_ref, osre_ref, osim_ref,
                ext_ref, bur_ref, bui_ref, sre_ref, sim_ref):
    i = pl.program_id(0)
    rows = lt * batch
    hist = POOL_BUF * batch
    d_pool = ext_ref.shape[1]
    n_state = bur_ref.shape[1]
    gw = d_pool // len(POOL_WINDOWS)
    n_chunk = bblk_ref.shape[0]
    ch_in = bblk_ref.shape[1]
    ch_st = bblk_ref.shape[2] // 2

    @pl.when(i == 0)
    def _():
        ext_ref[0:hist, :] = pool0_ref[...]
        sre_ref[...] = s0re_ref[...]
        sim_ref[...] = s0im_ref[...]

    x = h_ref[...]
    if x.ndim == 3:
        x = jnp.swapaxes(x, 0, 1).reshape(rows, x.shape[2])
    if first:
        x = _layernorm(x, lng_ref[...], lnb_ref[...])
    u = _dot(x.astype(BF16), win_ref[...])
    u_pool = u[:, :d_pool]
    u_ssm = u[:, d_pool:]
    ext_ref[hist:hist + rows, :] = u_pool

    t_loc = lax.broadcasted_iota(jnp.int32, (lt, batch, gw), 0).reshape(rows, gw)
    pos1 = t_loc + (start_pos + 1) + i * lt
    pool_outs = []
    for g, w in enumerate(POOL_WINDOWS):
        s = ext_ref[:, g * gw:(g + 1) * gw]
        width = 1
        while width < w:
            n = s.shape[0]
            s = s[width * batch:, :] + s[:n - width * batch, :]
            width *= 2
        wsum = s[s.shape[0] - rows:, :]
        cnt = jnp.minimum(pos1, w).astype(F32)
        pooled = wsum / cnt - u_pool[:, g * gw:(g + 1) * gw]
        pool_outs.append(_dot(pooled.astype(BF16), wpool_ref[g]))
    pool_out = jnp.concatenate(pool_outs, axis=1) * pscale_ref[...]

    for c in range(n_chunk):
        bu = _dot(u_ssm[:, c * ch_in:(c + 1) * ch_in].astype(BF16), bblk_ref[c])
        bur_ref[:, c * ch_st:(c + 1) * ch_st] = bu[:, :ch_st]
        bui_ref[:, c * ch_st:(c + 1) * ch_st] = bu[:, ch_st:]

    n_bt = batch // SUBLANES
    for q in range(n_state // SCAN_LANES):
        lanes = slice(q * SCAN_LANES, (q + 1) * SCAN_LANES)
        lr = jnp.broadcast_to(lre_ref[:, lanes], (SUBLANES, SCAN_LANES))
        li = jnp.broadcast_to(lim_ref[:, lanes], (SUBLANES, SCAN_LANES))

        def scan_tile(bt, carry, lanes=lanes, lr=lr, li=li):
            r0 = pl.multiple_of(bt * SUBLANES, SUBLANES)
            srows = pl.ds(r0, SUBLANES)

            def step(t, hc):
                hr, hi = hc
                trows = pl.ds(pl.multiple_of(t * batch + r0, SUBLANES), SUBLANES)
                nhr = lr * hr - li * hi + bur_ref[trows, lanes]
                nhi = lr * hi + li * hr + bui_ref[trows, lanes]
                bur_ref[trows, lanes] = nhr
                bui_ref[trows, lanes] = nhi
                return nhr, nhi

            hr, hi = lax.fori_loop(0, lt, step, (sre_ref[srows, lanes], sim_ref[srows, lanes]),
                                   unroll=min(lt, 4))
            sre_ref[srows, lanes] = hr
            sim_ref[srows, lanes] = hi
            return carry

        if n_bt == 1:
            scan_tile(0, 0)
        else:
            lax.fori_loop(0, n_bt, scan_tile, 0)

    ys = []
    for c in range(n_chunk):
        st = slice(c * ch_st, (c + 1) * ch_st)
        hcat = jnp.concatenate([bur_ref[:, st], bui_ref[:, st]], axis=1).astype(BF16)
        ys.append(_dot(hcat, cblk_ref[c]))
    y = jnp.concatenate(ys, axis=1) + dskip_ref[...] * u_ssm
    z = jax.nn.gelu(y)
    ssm_out = z * jax.nn.sigmoid(_dot(z.astype(BF16), wglu_ref[...]) + bglu_ref[...])

    mix = _dot(jnp.concatenate([pool_out, ssm_out], axis=1).astype(BF16), wout_ref[...])
    out_ref[...] = _layernorm(alpha * x + mix, ln1g_ref[...], ln1b_ref[...])

    @pl.when(i == nblk - 1)
    def _():
        opool_ref[...] = ext_ref[rows:rows + hist, :]
        osre_ref[...] = sre_ref[...]
        osim_ref[...] = sim_ref[...]

    if nblk > 1:
        @pl.when(i < nblk - 1)
        def _():
            ext_ref[0:hist, :] = ext_ref[rows:rows + hist, :]


def _mixer(layer, first, alpha, batch, start_pos, h, pool0, s0re, s0im, wts):
    if h.ndim == 3:
        seq, d_model = h.shape[1:]
        t_rows = seq * batch
    else:
        t_rows, d_model = h.shape
        seq = t_rows // batch
    lt = max(1, min(seq, MIXER_ROWS // batch))
    assert seq % lt == 0 and batch % SUBLANES == 0
    nblk = seq // lt
    rows = lt * batch
    hist = POOL_BUF * batch
    assert nblk == 1 or rows >= hist
    d_pool = pool0.shape[1]
    n_state = s0re.shape[1]
    assert n_state % SCAN_LANES == 0

    once = pl.Buffered(1)

    def lw(arr):
        nd = arr.ndim - 1
        return pl.BlockSpec((None,) + arr.shape[1:], lambda i, _nd=nd: (layer,) + (0,) * _nd,
                            pipeline_mode=once)

    def full(arr, **kw):
        nd = arr.ndim
        return pl.BlockSpec(arr.shape, lambda i, _nd=nd: (0,) * _nd, **kw)

    names = ["ln_in_g", "ln_in_b", "w_in", "w_pool", "pool_scale", "bblk", "lam_re", "lam_im",
             "cblk", "ssm_d", "w_glu", "b_glu", "w_out", "ln1_g", "ln1_b"]
    w_args = [wts[n] for n in names]
    w_specs = [full(wts[n], pipeline_mode=once) if n.startswith("ln_in") else lw(wts[n]) for n in names]
    if h.ndim == 3:
        h_spec = pl.BlockSpec((batch, lt, d_model), lambda i: (0, i, 0))
    else:
        h_spec = pl.BlockSpec((rows, d_model), lambda i: (i, 0))

    body = functools.partial(_mixer_body, nblk, batch, lt, start_pos, alpha, first)
    return pl.pallas_call(
        body,
        grid=(nblk,),
        in_specs=[h_spec] + w_specs
        + [full(pool0, pipeline_mode=once), full(s0re, pipeline_mode=once), full(s0im, pipeline_mode=once)],
        out_specs=(pl.BlockSpec((rows, d_model), lambda i: (i, 0)),
                   full(pool0), full(s0re), full(s0im)),
        out_shape=(jax.ShapeDtypeStruct((t_rows, d_model), F32), jax.ShapeDtypeStruct(pool0.shape, F32),
                   jax.ShapeDtypeStruct(s0re.shape, F32), jax.ShapeDtypeStruct(s0im.shape, F32)),
        scratch_shapes=[pltpu.VMEM((hist + rows, d_pool), F32),
                        pltpu.VMEM((rows, n_state), F32), pltpu.VMEM((rows, n_state), F32),
                        pltpu.VMEM((batch, n_state), F32), pltpu.VMEM((batch, n_state), F32)],
        compiler_params=pltpu.CompilerParams(dimension_semantics=("arbitrary",),
                                             vmem_limit_bytes=VMEM_LIMIT),
        name="mixer",
    )(h, *w_args, pool0, s0re, s0im)


def _route_transposed(s_t, sel_t):
    ng, pg = N_EXPERT_GROUPS, EXPERTS_PER_GROUP
    tn = s_t.shape[1]
    s = [s_t[j * ng:(j + 1) * ng, :] for j in range(pg)]
    v = [sel_t[j * ng:(j + 1) * ng, :] for j in range(pg)]
    neg = jnp.float32(-jnp.inf)

    m1 = v[0]
    for j in range(1, pg):
        m1 = jnp.maximum(m1, v[j])
    first_arg = jnp.full(m1.shape, pg, jnp.int32)
    for j in reversed(range(pg)):
        first_arg = jnp.where(v[j] == m1, j, first_arg)
    m2 = jnp.full(m1.shape, neg, F32)
    for j in range(pg):
        m2 = jnp.maximum(m2, jnp.where(first_arg == j, neg, v[j]))
    gscore = m1 + m2

    gio = lax.broadcasted_iota(jnp.int32, (ng, tn), 0)
    lower = [gio > g2 for g2 in range(ng)]
    grank = jnp.zeros((ng, tn), jnp.int32)
    for g2 in range(ng):
        row = jnp.broadcast_to(gscore[g2:g2 + 1, :], (ng, tn))
        grank += ((row > gscore) | ((row == gscore) & lower[g2])).astype(jnp.int32)
    gmask = grank < TOPK_GROUPS

    mk = [jnp.where(gmask, v[j], neg) for j in range(pg)]
    eidx = [gio * pg + j for j in range(pg)]
    taken = [jnp.zeros((ng, tn), jnp.bool_) for _ in range(pg)]
    for _ in range(TOP_K):
        m = mk[0]
        for j in range(1, pg):
            m = jnp.maximum(m, mk[j])
        m = jnp.broadcast_to(jnp.max(m, axis=0, keepdims=True), (ng, tn))
        key = [jnp.where(mk[j] == m, eidx[j], ng * pg) for j in range(pg)]
        kmin = key[0]
        for j in range(1, pg):
            kmin = jnp.minimum(kmin, key[j])
        kmin = jnp.broadcast_to(jnp.min(kmin, axis=0, keepdims=True), (ng, tn))
        for j in range(pg):
            pick = key[j] == kmin
            taken[j] = taken[j] | pick
            mk[j] = jnp.where(pick, neg, mk[j])

    selw = [jnp.where(taken[j], s[j], 0.0) for j in range(pg)]
    tot = selw[0]
    for j in range(1, pg):
        tot = tot + selw[j]
    denom = jnp.sum(tot, axis=0, keepdims=True)
    gates = [selw[j] / denom * ROUTED_SCALE for j in range(pg)]
    pad = jnp.zeros((s_t.shape[0] - ng * pg, tn), F32)
    return jnp.concatenate(gates + [pad], axis=0)


def _ffn_body(nchunk, pairs, alpha,
              x_ref, p_ref, wr_ref, br_ref, wg_ref, wu_ref, wd_ref, wsgu_ref, wsd_ref,
              wpg_ref, bpg_ref, wpp_ref, g_ref, b_ref,
              out_ref, xb_ref, gate_ref, acc_ref):
    c = pl.program_id(1)
    pg = EXPERTS_PER_GROUP

    @pl.when(c == 0)
    def _():
        x = x_ref[...]
        xh = x.astype(BF16)
        xb_ref[...] = xh
        xl = (x - xh.astype(F32)).astype(BF16)
        w = wr_ref[...]
        wh = w.astype(BF16)
        wl = (w - wh.astype(F32)).astype(BF16)
        logits = _dot(xh, wh) + (_dot(xl, wh) + _dot(xh, wl))
        scores = jax.nn.sigmoid(logits)
        sel = scores + br_ref[...]
        gate_ref[...] = _route_transposed(scores.T, sel.T).T

        d_sh = wsd_ref.shape[0]
        hs = _dot(xh, wsgu_ref[...])
        hsg = hs[:, :d_sh]
        sh = (hsg * jax.nn.sigmoid(hsg) * hs[:, d_sh:]).astype(BF16)
        shared = _dot(sh, wsd_ref[...])
        pgate = jax.nn.sigmoid(_dot(xh, wpg_ref[...]) + bpg_ref[...])
        p = p_ref[...]
        if p.ndim == 3:
            p = jnp.swapaxes(p, 0, 1).reshape(x.shape[0], p.shape[2])
        ple = pgate * _dot(p.astype(BF16), wpp_ref[...])
        acc_ref[...] = alpha * x + shared + ple

    xb = xb_ref[...]
    gates = gate_ref[...]
    acts = []
    for i in range(pairs):
        wcat = jnp.concatenate([wg_ref[2 * i], wg_ref[2 * i + 1], wu_ref[2 * i], wu_ref[2 * i + 1]], axis=1)
        hgu = _dot(xb, wcat)
        half = hgu.shape[1] // 2
        hg = hgu[:, :half]
        hu = hgu[:, half:]
        e0 = (c * pairs + i) * 2
        lane0 = (e0 % pg) * N_EXPERT_GROUPS + e0 // pg
        g0 = jnp.take_along_axis(gates, jnp.full(gates.shape, lane0, jnp.int32), axis=1)
        g1 = jnp.take_along_axis(gates, jnp.full(gates.shape, lane0 + N_EXPERT_GROUPS, jnp.int32), axis=1)
        reps = half // 2 // LANES
        gexp = jnp.concatenate([g0] * reps + [g1] * reps, axis=1)
        acts.append((hg * jax.nn.sigmoid(hg) * hu * gexp).astype(BF16))
    wd = wd_ref[...]
    acc_ref[...] += _dot(jnp.concatenate(acts, axis=1), wd.reshape(wd.shape[0] * wd.shape[1], wd.shape[2]))

    @pl.when(c == nchunk - 1)
    def _():
        y = _layernorm(acc_ref[...], g_ref[...], b_ref[...])
        if len(out_ref.shape) == 3:
            batch = out_ref.shape[0]
            y = jnp.swapaxes(y.reshape(y.shape[0] // batch, batch, y.shape[1]), 0, 1)
        out_ref[...] = y


def _ffn(layer, alpha, h, p_all, wts, out_batch=None):
    t_rows, d_model = h.shape
    tb = min(FFN_ROWS, t_rows)
    assert t_rows % tb == 0 and tb % LANES == 0
    n_exp, _, d_exp = wts["w_gate"].shape[1:]
    pairs = min(FFN_PAIRS, n_exp // 2)
    assert n_exp % (2 * pairs) == 0 and EXPERTS_PER_GROUP % 2 == 0
    nchunk = n_exp // (2 * pairs)
    d_ple = p_all.shape[-1]
    if p_all.ndim == 4:
        batch = p_all.shape[1]
        p_spec = pl.BlockSpec((None, batch, tb // batch, d_ple), lambda i, c: (layer, 0, i, 0))
    else:
        p_spec = pl.BlockSpec((None, tb, d_ple), lambda i, c: (layer, i, 0))

    def lw(arr):
        nd = arr.ndim - 1
        return pl.BlockSpec((None,) + arr.shape[1:], lambda i, c, _nd=nd: (layer,) + (0,) * _nd,
                            pipeline_mode=pl.Buffered(1))

    if out_batch is None:
        out_spec = pl.BlockSpec((tb, d_model), lambda i, c: (i, 0))
        out_shape = jax.ShapeDtypeStruct(h.shape, F32)
    else:
        out_spec = pl.BlockSpec((out_batch, tb // out_batch, d_model), lambda i, c: (0, i, 0))
        out_shape = jax.ShapeDtypeStruct((out_batch, t_rows // out_batch, d_model), F32)

    body = functools.partial(_ffn_body, nchunk, pairs, alpha)
    return pl.pallas_call(
        body,
        grid=(t_rows // tb, nchunk),
        in_specs=[
            pl.BlockSpec((tb, d_model), lambda i, c: (i, 0)),
            p_spec,
            lw(wts["w_router"]), lw(wts["b_router"]),
            pl.BlockSpec((None, 2 * pairs, d_model, d_exp), lambda i, c: (layer, c, 0, 0)),
            pl.BlockSpec((None, 2 * pairs, d_model, d_exp), lambda i, c: (layer, c, 0, 0)),
            pl.BlockSpec((None, 2 * pairs, d_exp, d_model), lambda i, c: (layer, c, 0, 0)),
            lw(wts["ws_gu"]), lw(wts["ws_down"]), lw(wts["w_pg"]), lw(wts["b_pg"]), lw(wts["w_pp"]),
            lw(wts["ln2_g"]), lw(wts["ln2_b"]),
        ],
        out_specs=out_spec,
        out_shape=out_shape,
        scratch_shapes=[pltpu.VMEM((tb, d_model), BF16), pltpu.VMEM((tb, LANES), F32),
                        pltpu.VMEM((tb, d_model), F32)],
        compiler_params=pltpu.CompilerParams(dimension_semantics=("parallel", "arbitrary"),
                                             vmem_limit_bytes=VMEM_LIMIT),
        name="ffn",
    )(h, p_all, wts["w_router"], wts["b_router"], wts["w_gate"], wts["w_up"], wts["w_down"], wts["ws_gu"],
      wts["ws_down"], wts["w_pg"], wts["b_pg"], wts["w_pp"], wts["ln2_g"], wts["ln2_b"])


def _block_diag(blocks, per_chunk):
    depth, groups, a, b = blocks.shape
    nch = groups // per_chunk
    eye = jnp.eye(per_chunk, dtype=blocks.dtype)
    x = blocks.reshape(depth, nch, per_chunk, a, 1, b) * eye[None, None, :, None, :, None]
    return x.reshape(depth, nch, per_chunk * a, per_chunk * b)


def _prepare_weights(ln_in_g, ln_in_b, w_in, w_pool, pool_scale, ssm_a_re, ssm_a_im, ssm_log_dt,
                     ssm_b_re, ssm_b_im, ssm_c_re, ssm_c_im, ssm_d, w_glu, b_glu, w_out, ln1_g, ln1_b,
                     w_router, b_router, w_gate, w_up, w_down, ws_gate, ws_up, ws_down,
                     w_pg, b_pg, w_pp, ln2_g, ln2_b):
    depth, d_model, _ = w_in.shape
    groups, n_st = ssm_a_re.shape[1:]
    row = lambda a: a.reshape(depth, 1, a.shape[-1])

    log_dt_b = jnp.broadcast_to(ssm_log_dt[..., None], (depth, groups, n_st)).reshape(depth * groups, n_st)
    bt_re = ssm_b_re.transpose(0, 1, 3, 2).reshape(depth * groups, SSM_H, n_st)
    bt_im = ssm_b_im.transpose(0, 1, 3, 2).reshape(depth * groups, SSM_H, n_st)
    lam_re, lam_im, bb_re, bb_im = _s5_prep(
        ssm_a_re.reshape(depth * groups, n_st), ssm_a_im.reshape(depth * groups, n_st), log_dt_b, bt_re, bt_im)
    per_chunk = LANES // SSM_H
    bb_re = bb_re.reshape(depth, groups, SSM_H, n_st)
    bb_im = bb_im.reshape(depth, groups, SSM_H, n_st)
    bblk = jnp.concatenate([_block_diag(bb_re, per_chunk), _block_diag(bb_im, per_chunk)], axis=-1).astype(BF16)
    ct_re = ssm_c_re.transpose(0, 1, 3, 2)
    ct_im = ssm_c_im.transpose(0, 1, 3, 2)
    cblk = jnp.concatenate([_block_diag(ct_re, per_chunk), -_block_diag(ct_im, per_chunk)], axis=-2).astype(BF16)

    n_exp = w_gate.shape[1]
    ng, pg = N_EXPERT_GROUPS, EXPERTS_PER_GROUP

    def slot_cols(a):
        lead = a.shape[:-1]
        a = a.reshape(lead + (ng, pg)).swapaxes(-1, -2).reshape(lead + (n_exp,))
        return jnp.pad(a, [(0, 0)] * len(lead) + [(0, LANES - n_exp)])

    return {
        "ln_in_g": ln_in_g.reshape(1, d_model), "ln_in_b": ln_in_b.reshape(1, d_model),
        "w_in": w_in.astype(BF16), "w_pool": w_pool.astype(BF16), "pool_scale": row(pool_scale),
        "bblk": bblk, "lam_re": lam_re.reshape(depth, 1, groups * n_st),
        "lam_im": lam_im.reshape(depth, 1, groups * n_st), "cblk": cblk,
        "ssm_d": row(ssm_d), "w_glu": w_glu.astype(BF16), "b_glu": row(b_glu),
        "w_out": w_out.astype(BF16), "ln1_g": row(ln1_g), "ln1_b": row(ln1_b),
        "w_router": slot_cols(w_router), "b_router": row(slot_cols(b_router)),
        "w_gate": w_gate.astype(BF16), "w_up": w_up.astype(BF16), "w_down": w_down.astype(BF16),
        "ws_gu": jnp.concatenate([ws_gate, ws_up], axis=-1).astype(BF16), "ws_down": ws_down.astype(BF16),
        "w_pg": w_pg.astype(BF16), "b_pg": row(b_pg), "w_pp": w_pp.astype(BF16),
        "ln2_g": row(ln2_g), "ln2_b": row(ln2_b),
    }


def _time_major(x):
    b, s, d = x.shape
    return x.transpose(1, 0, 2).reshape(s * b, d)


def _batch_major(x, batch):
    t, d = x.shape
    return x.reshape(t // batch, batch, d).transpose(1, 0, 2)


def kernel(x_prompt, x_sample, state_pool, state_ssm_re, state_ssm_im, p_prompt, p_sample, ln_in_g, ln_in_b, w_in, w_pool, pool_scale, ssm_a_re, ssm_a_im, ssm_log_dt, ssm_b_re, ssm_b_im, ssm_c_re, ssm_c_im, ssm_d, w_glu, b_glu, w_out, ln1_g, ln1_b, w_router, b_router, w_gate, w_up, w_down, ws_gate, ws_up, ws_down, w_pg, b_pg, w_pp, ln2_g, ln2_b):
    depth = w_in.shape[0]
    alpha = (2.0 * depth) ** 0.25
    nb_p, nb_s = x_prompt.shape[0], x_sample.shape[0]
    d_pool = state_pool.shape[-1]
    groups, n_st = state_ssm_re.shape[-2:]
    n_state = groups * n_st

    wts = _prepare_weights(ln_in_g, ln_in_b, w_in, w_pool, pool_scale, ssm_a_re, ssm_a_im, ssm_log_dt,
                           ssm_b_re, ssm_b_im, ssm_c_re, ssm_c_im, ssm_d, w_glu, b_glu, w_out, ln1_g, ln1_b,
                           w_router, b_router, w_gate, w_up, w_down, ws_gate, ws_up, ws_down,
                           w_pg, b_pg, w_pp, ln2_g, ln2_b)

    hp = x_prompt
    pp = p_prompt
    hs = _time_major(x_sample)
    ps = p_sample.transpose(0, 2, 1, 3).reshape(depth, hs.shape[0], p_sample.shape[-1])
    pool0_p = jnp.zeros((POOL_BUF * nb_p, d_pool), F32)
    s0_p = jnp.zeros((nb_p, n_state), F32)
    pool0_s = state_pool.transpose(0, 2, 1, 3).reshape(depth, POOL_BUF * nb_s, d_pool)
    s0re_s = state_ssm_re.reshape(depth, nb_s, n_state)
    s0im_s = state_ssm_im.reshape(depth, nb_s, n_state)

    outs = {k: [] for k in ("pool_p", "re_p", "im_p", "pool_s", "re_s", "im_s")}
    for l in range(depth):
        hp, bp, rp, ip = _mixer(l, l == 0, alpha, nb_p, 0, hp, pool0_p, s0_p, s0_p, wts)
        hp = _ffn(l, alpha, hp, pp, wts, out_batch=nb_p if l == depth - 1 else None)
        hs, bs, rs, is_ = _mixer(l, l == 0, alpha, nb_s, PAST_LEN, hs, pool0_s[l], s0re_s[l], s0im_s[l], wts)
        hs = _ffn(l, alpha, hs, ps, wts)
        outs["pool_p"].append(_batch_major(bp, nb_p))
        outs["re_p"].append(rp.reshape(nb_p, groups, n_st))
        outs["im_p"].append(ip.reshape(nb_p, groups, n_st))
        outs["pool_s"].append(_batch_major(bs, nb_s))
        outs["re_s"].append(rs.reshape(nb_s, groups, n_st))
        outs["im_s"].append(is_.reshape(nb_s, groups, n_st))

    return (hp, _batch_major(hs, nb_s),
            jnp.stack(outs["pool_p"]), jnp.stack(outs["re_p"]), jnp.stack(outs["im_p"]),
            jnp.stack(outs["pool_s"]), jnp.stack(outs["re_s"]), jnp.stack(outs["im_s"]))
```

```python
import functools

import jax
import jax.numpy as jnp
from jax import lax
from jax.experimental import pallas as pl
from jax.experimental.pallas import tpu as pltpu

F32 = jnp.float32
BF16 = jnp.bfloat16

POOL_WINDOWS = (2, 4, 8, 16)
POOL_BUF = max(POOL_WINDOWS) - 1
SSM_H = 16
SSM_STATE = 64
N_EXPERT_GROUPS = 8
EXPERTS_PER_GROUP = 8
TOPK_GROUPS = 4
TOP_K = 8
ROUTED_SCALE = 2.5
PAST_LEN = 16384
LN_EPS = 1e-5

LANES = 128
SUBLANES = 8
SCAN_LANES = 512
MIXER_ROWS = 1024
FFN_ROWS = 1024
FFN_PAIRS = 4
VMEM_LIMIT = 56 * 1024 * 1024


def _dot(a, b):
    return jnp.dot(a, b, preferred_element_type=F32)


def _layernorm(x, g, b):
    mu = jnp.mean(x, axis=-1, keepdims=True)
    xc = x - mu
    var = jnp.mean(xc * xc, axis=-1, keepdims=True)
    return xc * lax.rsqrt(var + LN_EPS) * g + b


def _s5_prep_body(are_ref, aim_ref, ldt_ref, bre_ref, bim_ref,
                  lre_ref, lim_ref, bbre_ref, bbim_ref):
    a_re = are_ref[...]
    a_im = aim_ref[...]
    dt = jnp.exp(ldt_ref[...])
    mag = jnp.exp(a_re * dt)
    lam_re = mag * jnp.cos(a_im * dt)
    lam_im = mag * jnp.sin(a_im * dt)
    den = a_re * a_re + a_im * a_im
    nr = lam_re - 1.0
    ni = lam_im
    f_re = ((nr * a_re + ni * a_im) / den)[:, None, :]
    f_im = ((ni * a_re - nr * a_im) / den)[:, None, :]
    b_re = bre_ref[...]
    b_im = bim_ref[...]
    lre_ref[...] = lam_re
    lim_ref[...] = lam_im
    bbre_ref[...] = f_re * b_re - f_im * b_im
    bbim_ref[...] = f_re * b_im + f_im * b_re


def _s5_prep(a_re, a_im, log_dt_b, bt_re, bt_im):
    n, p = a_re.shape
    return pl.pallas_call(
        _s5_prep_body,
        out_shape=(jax.ShapeDtypeStruct((n, p), F32), jax.ShapeDtypeStruct((n, p), F32),
                   jax.ShapeDtypeStruct(bt_re.shape, F32), jax.ShapeDtypeStruct(bt_im.shape, F32)),
        name="s5_prep",
    )(a_re, a_im, log_dt_b, bt_re, bt_im)


def _mixer_body(nblk, batch, lt, start_pos, alpha, first,
                h_ref, lng_ref, lnb_ref, win_ref, wpool_ref, pscale_ref, bblk_ref,
                lre_ref, lim_ref, cblk_ref, dskip_ref, wglu_ref, bglu_ref, wout_ref,
                ln1g_ref, ln1b_ref, pool0_ref, s0re_ref, s0im_ref,
                out_ref, opool_ref, osre_ref, osim_ref,
                ext_ref, bur_ref, bui_ref, sre_ref, sim_ref):
    i = pl.program_id(0)
    rows = lt * batch
    hist = POOL_BUF * batch
    d_pool = ext_ref.shape[1]
    n_state = bur_ref.shape[1]
    gw = d_pool // len(POOL_WINDOWS)
    n_chunk = bblk_ref.shape[0]
    ch_in = bblk_ref.shape[1]
    ch_st = bblk_ref.shape[2] // 2

    @pl.when(i == 0)
    def _():
        ext_ref[0:hist, :] = pool0_ref[...]
        sre_ref[...] = s0re_ref[...]
        sim_ref[...] = s0im_ref[...]

    x = h_ref[...]
    if x.ndim == 3:
        x = jnp.swapaxes(x, 0, 1).reshape(rows, x.shape[2])
    if first:
        x = _layernorm(x, lng_ref[...], lnb_ref[...])
    u = _dot(x.astype(BF16), win_ref[...])
    u_pool = u[:, :d_pool]
    u_ssm = u[:, d_pool:]
    ext_ref[hist:hist + rows, :] = u_pool

    t_loc = lax.broadcasted_iota(jnp.int32, (lt, batch, gw), 0).reshape(rows, gw)
    pos1 = t_loc + (start_pos + 1) + i * lt
    pool_outs = []
    for g, w in enumerate(POOL_WINDOWS):
        s = ext_ref[:, g * gw:(g + 1) * gw]
        width = 1
        while width < w:
            n = s.shape[0]
            s = s[width * batch:, :] + s[:n - width * batch, :]
            width *= 2
        wsum = s[s.shape[0] - rows:, :]
        cnt = jnp.minimum(pos1, w).astype(F32)
        pooled = wsum / cnt - u_pool[:, g * gw:(g + 1) * gw]
        pool_outs.append(_dot(pooled.astype(BF16), wpool_ref[g]))
    pool_out = jnp.concatenate(pool_outs, axis=1) * pscale_ref[...]

    for c in range(n_chunk):
        bu = _dot(u_ssm[:, c * ch_in:(c + 1) * ch_in].astype(BF16), bblk_ref[c])
        bur_ref[:, c * ch_st:(c + 1) * ch_st] = bu[:, :ch_st]
        bui_ref[:, c * ch_st:(c + 1) * ch_st] = bu[:, ch_st:]

    n_bt = batch // SUBLANES
    for q in range(n_state // SCAN_LANES):
        lanes = slice(q * SCAN_LANES, (q + 1) * SCAN_LANES)
        lr = jnp.broadcast_to(lre_ref[:, lanes], (SUBLANES, SCAN_LANES))
        li = jnp.broadcast_to(lim_ref[:, lanes], (SUBLANES, SCAN_LANES))

        def scan_tile(bt, carry, lanes=lanes, lr=lr, li=li):
            r0 = pl.multiple_of(bt * SUBLANES, SUBLANES)
            srows = pl.ds(r0, SUBLANES)

            def step(t, hc):
                hr, hi = hc
                trows = pl.ds(pl.multiple_of(t * batch + r0, SUBLANES), SUBLANES)
                nhr = lr * hr - li * hi + bur_ref[trows, lanes]
                nhi = lr * hi + li * hr + bui_ref[trows, lanes]
                bur_ref[trows, lanes] = nhr
                bui_ref[trows, lanes] = nhi
                return nhr, nhi

            hr, hi = lax.fori_loop(0, lt, step, (sre_ref[srows, lanes], sim_ref[srows, lanes]),
                                   unroll=min(lt, 4))
            sre_ref[srows, lanes] = hr
            sim_ref[srows, lanes] = hi
            return carry

        if n_bt == 1:
            scan_tile(0, 0)
        else:
            lax.fori_loop(0, n_bt, scan_tile, 0)

    ys = []
    for c in range(n_chunk):
        st = slice(c * ch_st, (c + 1) * ch_st)
        hcat = jnp.concatenate([bur_ref[:, st], bui_ref[:, st]], axis=1).astype(BF16)
        ys.append(_dot(hcat, cblk_ref[c]))
    y = jnp.concatenate(ys, axis=1) + dskip_ref[...] * u_ssm
    z = jax.nn.gelu(y)
    ssm_out = z * jax.nn.sigmoid(_dot(z.astype(BF16), wglu_ref[...]) + bglu_ref[...])

    mix = _dot(jnp.concatenate([pool_out, ssm_out], axis=1).astype(BF16), wout_ref[...])
    out_ref[...] = _layernorm(alpha * x + mix, ln1g_ref[...], ln1b_ref[...])

    @pl.when(i == nblk - 1)
    def _():
        opool_ref[...] = ext_ref[rows:rows + hist, :]
        osre_ref[...] = sre_ref[...]
        osim_ref[...] = sim_ref[...]

    if nblk > 1:
        @pl.when(i < nblk - 1)
        def _():
            ext_ref[0:hist, :] = ext_ref[rows:rows + hist, :]


def _mixer(layer, first, alpha, batch, start_pos, h, pool0, s0re, s0im, wts):
    if h.ndim == 3:
        seq, d_model = h.shape[1:]
        t_rows = seq * batch
    else:
        t_rows, d_model = h.shape
        seq = t_rows // batch
    lt = max(1, min(seq, MIXER_ROWS // batch))
    assert seq % lt == 0 and batch % SUBLANES == 0
    nblk = seq // lt
    rows = lt * batch
    hist = POOL_BUF * batch
    assert nblk == 1 or rows >= hist
    d_pool = pool0.shape[1]
    n_state = s0re.shape[1]
    assert n_state % SCAN_LANES == 0

    once = pl.Buffered(1)

    def lw(arr):
        nd = arr.ndim - 1
        return pl.BlockSpec((None,) + arr.shape[1:], lambda i, _nd=nd: (layer,) + (0,) * _nd,
                            pipeline_mode=once)

    def full(arr, **kw):
        nd = arr.ndim
        return pl.BlockSpec(arr.shape, lambda i, _nd=nd: (0,) * _nd, **kw)

    names = ["ln_in_g", "ln_in_b", "w_in", "w_pool", "pool_scale", "bblk", "lam_re", "lam_im",
             "cblk", "ssm_d", "w_glu", "b_glu", "w_out", "ln1_g", "ln1_b"]
    w_args = [wts[n] for n in names]
    w_specs = [full(wts[n], pipeline_mode=once) if n.startswith("ln_in") else lw(wts[n]) for n in names]
    if h.ndim == 3:
        h_spec = pl.BlockSpec((batch, lt, d_model), lambda i: (0, i, 0))
    else:
        h_spec = pl.BlockSpec((rows, d_model), lambda i: (i, 0))

    body = functools.partial(_mixer_body, nblk, batch, lt, start_pos, alpha, first)
    return pl.pallas_call(
        body,
        grid=(nblk,),
        in_specs=[h_spec] + w_specs
        + [full(pool0, pipeline_mode=once), full(s0re, pipeline_mode=once), full(s0im, pipeline_mode=once)],
        out_specs=(pl.BlockSpec((rows, d_model), lambda i: (i, 0)),
                   full(pool0), full(s0re), full(s0im)),
        out_shape=(jax.ShapeDtypeStruct((t_rows, d_model), F32), jax.ShapeDtypeStruct(pool0.shape, F32),
                   jax.ShapeDtypeStruct(s0re.shape, F32), jax.ShapeDtypeStruct(s0im.shape, F32)),
        scratch_shapes=[pltpu.VMEM((hist + rows, d_pool), F32),
                        pltpu.VMEM((rows, n_state), F32), pltpu.VMEM((rows, n_state), F32),
                        pltpu.VMEM((batch, n_state), F32), pltpu.VMEM((batch, n_state), F32)],
        compiler_params=pltpu.CompilerParams(dimension_semantics=("arbitrary",),
                                             vmem_limit_bytes=VMEM_LIMIT),
        name="mixer",
    )(h, *w_args, pool0, s0re, s0im)


def _route_transposed(s_t, sel_t):
    ng, pg = N_EXPERT_GROUPS, EXPERTS_PER_GROUP
    tn = s_t.shape[1]
    s = [s_t[j * ng:(j + 1) * ng, :] for j in range(pg)]
    v = [sel_t[j * ng:(j + 1) * ng, :] for j in range(pg)]
    neg = jnp.float32(-jnp.inf)

    m1 = v[0]
    for j in range(1, pg):
        m1 = jnp.maximum(m1, v[j])
    first_arg = jnp.full(m1.shape, pg, jnp.int32)
    for j in reversed(range(pg)):
        first_arg = jnp.where(v[j] == m1, j, first_arg)
    m2 = jnp.full(m1.shape, neg, F32)
    for j in range(pg):
        m2 = jnp.maximum(m2, jnp.where(first_arg == j, neg, v[j]))
    gscore = m1 + m2

    gio = lax.broadcasted_iota(jnp.int32, (ng, tn), 0)
    lower = [gio > g2 for g2 in range(ng)]
    grank = jnp.zeros((ng, tn), jnp.int32)
    for g2 in range(ng):
        row = jnp.broadcast_to(gscore[g2:g2 + 1, :], (ng, tn))
        grank += ((row > gscore) | ((row == gscore) & lower[g2])).astype(jnp.int32)
    gmask = grank < TOPK_GROUPS

    mk = [jnp.where(gmask, v[j], neg) for j in range(pg)]
    eidx = [gio * pg + j for j in range(pg)]
    taken = [jnp.zeros((ng, tn), jnp.bool_) for _ in range(pg)]
    for _ in range(TOP_K):
        m = mk[0]
        for j in range(1, pg):
            m = jnp.maximum(m, mk[j])
        m = jnp.broadcast_to(jnp.max(m, axis=0, keepdims=True), (ng, tn))
        key = [jnp.where(mk[j] == m, eidx[j], ng * pg) for j in range(pg)]
        kmin = key[0]
        for j in range(1, pg):
            kmin = jnp.minimum(kmin, key[j])
        kmin = jnp.broadcast_to(jnp.min(kmin, axis=0, keepdims=True), (ng, tn))
        for j in range(pg):
            pick = key[j] == kmin
            taken[j] = taken[j] | pick
            mk[j] = jnp.where(pick, neg, mk[j])

    selw = [jnp.where(taken[j], s[j], 0.0) for j in range(pg)]
    tot = selw[0]
    for j in range(1, pg):
        tot = tot + selw[j]
    denom = jnp.sum(tot, axis=0, keepdims=True)
    gates = [selw[j] / denom * ROUTED_SCALE for j in range(pg)]
    pad = jnp.zeros((s_t.shape[0] - ng * pg, tn), F32)
    return jnp.concatenate(gates + [pad], axis=0)


def _ffn_body(nchunk, pairs, alpha,
              x_ref, p_ref, wr_ref, br_ref, wg_ref, wu_ref, wd_ref, wsgu_ref, wsd_ref,
              wpg_ref, bpg_ref, wpp_ref, g_ref, b_ref,
              out_ref, xb_ref, gate_ref, acc_ref):
    c = pl.program_id(1)
    pg = EXPERTS_PER_GROUP

    @pl.when(c == 0)
    def _():
        w = wr_ref[...]
        wh = w.astype(BF16)
        wl = (w - wh.astype(F32)).astype(BF16)
        d_sh = wsd_ref.shape[0]
        p_all = p_ref[...]
        if p_all.ndim == 3:
            p_all = jnp.swapaxes(p_all, 0, 1).reshape(x_ref.shape[0], p_all.shape[2])
        hr = x_ref.shape[0] // 2
        for r in range(2):
            rows = slice(r * hr, (r + 1) * hr)
            x = x_ref[rows, :]
            xh = x.astype(BF16)
            xb_ref[rows, :] = xh
            xl = (x - xh.astype(F32)).astype(BF16)
            logits = _dot(xh, wh) + (_dot(xl, wh) + _dot(xh, wl))
            scores = jax.nn.sigmoid(logits)
            sel = scores + br_ref[...]
            gate_ref[rows, :] = _route_transposed(scores.T, sel.T).T

            hs = _dot(xh, wsgu_ref[...])
            hsg = hs[:, :d_sh]
            sh = (hsg * jax.nn.sigmoid(hsg) * hs[:, d_sh:]).astype(BF16)
            shared = _dot(sh, wsd_ref[...])
            pgate = jax.nn.sigmoid(_dot(xh, wpg_ref[...]) + bpg_ref[...])
            ple = pgate * _dot(p_all[rows, :].astype(BF16), wpp_ref[...])
            acc_ref[rows, :] = alpha * x + shared + ple

    xb = xb_ref[...]
    gates = gate_ref[...]
    acts = []
    for i in range(pairs):
        wcat = jnp.concatenate([wg_ref[2 * i], wg_ref[2 * i + 1], wu_ref[2 * i], wu_ref[2 * i + 1]], axis=1)
        hgu = _dot(xb, wcat)
        half = hgu.shape[1] // 2
        hg = hgu[:, :half]
        hu = hgu[:, half:]
        e0 = (c * pairs + i) * 2
        lane0 = (e0 % pg) * N_EXPERT_GROUPS + e0 // pg
        g0 = jnp.take_along_axis(gates, jnp.full(gates.shape, lane0, jnp.int32), axis=1)
        g1 = jnp.take_along_axis(gates, jnp.full(gates.shape, lane0 + N_EXPERT_GROUPS, jnp.int32), axis=1)
        reps = half // 2 // LANES
        gexp = jnp.concatenate([g0] * reps + [g1] * reps, axis=1)
        acts.append((hg * jax.nn.sigmoid(hg) * hu * gexp).astype(BF16))
    wd = wd_ref[...]
    acc_ref[...] += _dot(jnp.concatenate(acts, axis=1), wd.reshape(wd.shape[0] * wd.shape[1], wd.shape[2]))

    @pl.when(c == nchunk - 1)
    def _():
        y = _layernorm(acc_ref[...], g_ref[...], b_ref[...])
        if len(out_ref.shape) == 3:
            batch = out_ref.shape[0]
            y = jnp.swapaxes(y.reshape(y.shape[0] // batch, batch, y.shape[1]), 0, 1)
        out_ref[...] = y


def _ffn(layer, alpha, h, p_all, wts, out_batch=None):
    t_rows, d_model = h.shape
    tb = min(FFN_ROWS, t_rows)
    assert t_rows % tb == 0 and tb % LANES == 0
    n_exp, _, d_exp = wts["w_gate"].shape[1:]
    pairs = min(FFN_PAIRS, n_exp // 2)
    assert n_exp % (2 * pairs) == 0 and EXPERTS_PER_GROUP % 2 == 0
    nchunk = n_exp // (2 * pairs)
    d_ple = p_all.shape[-1]
    if p_all.ndim == 4:
        batch = p_all.shape[1]
        p_spec = pl.BlockSpec((None, batch, tb // batch, d_ple), lambda i, c: (layer, 0, i, 0))
    else:
        p_spec = pl.BlockSpec((None, tb, d_ple), lambda i, c: (layer, i, 0))

    def lw(arr):
        nd = arr.ndim - 1
        return pl.BlockSpec((None,) + arr.shape[1:], lambda i, c, _nd=nd: (layer,) + (0,) * _nd,
                            pipeline_mode=pl.Buffered(1))

    if out_batch is None:
        out_spec = pl.BlockSpec((tb, d_model), lambda i, c: (i, 0))
        out_shape = jax.ShapeDtypeStruct(h.shape, F32)
    else:
        out_spec = pl.BlockSpec((out_batch, tb // out_batch, d_model), lambda i, c: (0, i, 0))
        out_shape = jax.ShapeDtypeStruct((out_batch, t_rows // out_batch, d_model), F32)

    body = functools.partial(_ffn_body, nchunk, pairs, alpha)
    return pl.pallas_call(
        body,
        grid=(t_rows // tb, nchunk),
        in_specs=[
            pl.BlockSpec((tb, d_model), lambda i, c: (i, 0)),
            p_spec,
            lw(wts["w_router"]), lw(wts["b_router"]),
            pl.BlockSpec((None, 2 * pairs, d_model, d_exp), lambda i, c: (layer, c, 0, 0)),
            pl.BlockSpec((None, 2 * pairs, d_model, d_exp), lambda i, c: (layer, c, 0, 0)),
            pl.BlockSpec((None, 2 * pairs, d_exp, d_model), lambda i, c: (layer, c, 0, 0)),
            lw(wts["ws_gu"]), lw(wts["ws_down"]), lw(wts["w_pg"]), lw(wts["b_pg"]), lw(wts["w_pp"]),
            lw(wts["ln2_g"]), lw(wts["ln2_b"]),
        ],
        out_specs=out_spec,
        out_shape=out_shape,
        scratch_shapes=[pltpu.VMEM((tb, d_model), BF16), pltpu.VMEM((tb, LANES), F32),
                        pltpu.VMEM((tb, d_model), F32)],
        compiler_params=pltpu.CompilerParams(dimension_semantics=("parallel", "arbitrary"),
                                             vmem_limit_bytes=VMEM_LIMIT),
        name="ffn",
    )(h, p_all, wts["w_router"], wts["b_router"], wts["w_gate"], wts["w_up"], wts["w_down"], wts["ws_gu"],
      wts["ws_down"], wts["w_pg"], wts["b_pg"], wts["w_pp"], wts["ln2_g"], wts["ln2_b"])


def _block_diag(blocks, per_chunk):
    depth, groups, a, b = blocks.shape
    nch = groups // per_chunk
    eye = jnp.eye(per_chunk, dtype=blocks.dtype)
    x = blocks.reshape(depth, nch, per_chunk, a, 1, b) * eye[None, None, :, None, :, None]
    return x.reshape(depth, nch, per_chunk * a, per_chunk * b)


def _prepare_weights(ln_in_g, ln_in_b, w_in, w_pool, pool_scale, ssm_a_re, ssm_a_im, ssm_log_dt,
                     ssm_b_re, ssm_b_im, ssm_c_re, ssm_c_im, ssm_d, w_glu, b_glu, w_out, ln1_g, ln1_b,
                     w_router, b_router, w_gate, w_up, w_down, ws_gate, ws_up, ws_down,
                     w_pg, b_pg, w_pp, ln2_g, ln2_b):
    depth, d_model, _ = w_in.shape
    groups, n_st = ssm_a_re.shape[1:]
    row = lambda a: a.reshape(depth, 1, a.shape[-1])

    log_dt_b = jnp.broadcast_to(ssm_log_dt[..., None], (depth, groups, n_st)).reshape(depth * groups, n_st)
    bt_re = ssm_b_re.transpose(0, 1, 3, 2).reshape(depth * groups, SSM_H, n_st)
    bt_im = ssm_b_im.transpose(0, 1, 3, 2).reshape(depth * groups, SSM_H, n_st)
    lam_re, lam_im, bb_re, bb_im = _s5_prep(
        ssm_a_re.reshape(depth * groups, n_st), ssm_a_im.reshape(depth * groups, n_st), log_dt_b, bt_re, bt_im)
    per_chunk = LANES // SSM_H
    bb_re = bb_re.reshape(depth, groups, SSM_H, n_st)
    bb_im = bb_im.reshape(depth, groups, SSM_H, n_st)
    bblk = jnp.concatenate([_block_diag(bb_re, per_chunk), _block_diag(bb_im, per_chunk)], axis=-1).astype(BF16)
    ct_re = ssm_c_re.transpose(0, 1, 3, 2)
    ct_im = ssm_c_im.transpose(0, 1, 3, 2)
    cblk = jnp.concatenate([_block_diag(ct_re, per_chunk), -_block_diag(ct_im, per_chunk)], axis=-2).astype(BF16)

    n_exp = w_gate.shape[1]
    ng, pg = N_EXPERT_GROUPS, EXPERTS_PER_GROUP

    def slot_cols(a):
        lead = a.shape[:-1]
        a = a.reshape(lead + (ng, pg)).swapaxes(-1, -2).reshape(lead + (n_exp,))
        return jnp.pad(a, [(0, 0)] * len(lead) + [(0, LANES - n_exp)])

    return {
        "ln_in_g": ln_in_g.reshape(1, d_model), "ln_in_b": ln_in_b.reshape(1, d_model),
        "w_in": w_in.astype(BF16), "w_pool": w_pool.astype(BF16), "pool_scale": row(pool_scale),
        "bblk": bblk, "lam_re": lam_re.reshape(depth, 1, groups * n_st),
        "lam_im": lam_im.reshape(depth, 1, groups * n_st), "cblk": cblk,
        "ssm_d": row(ssm_d), "w_glu": w_glu.astype(BF16), "b_glu": row(b_glu),
        "w_out": w_out.astype(BF16), "ln1_g": row(ln1_g), "ln1_b": row(ln1_b),
        "w_router": slot_cols(w_router), "b_router": row(slot_cols(b_router)),
        "w_gate": w_gate.astype(BF16), "w_up": w_up.astype(BF16), "w_down": w_down.astype(BF16),
        "ws_gu": jnp.concatenate([ws_gate, ws_up], axis=-1).astype(BF16), "ws_down": ws_down.astype(BF16),
        "w_pg": w_pg.astype(BF16), "b_pg": row(b_pg), "w_pp": w_pp.astype(BF16),
        "ln2_g": row(ln2_g), "ln2_b": row(ln2_b),
    }


def _time_major(x):
    b, s, d = x.shape
    return x.transpose(1, 0, 2).reshape(s * b, d)


def _batch_major(x, batch):
    t, d = x.shape
    return x.reshape(t // batch, batch, d).transpose(1, 0, 2)


def kernel(x_prompt, x_sample, state_pool, state_ssm_re, state_ssm_im, p_prompt, p_sample, ln_in_g, ln_in_b, w_in, w_pool, pool_scale, ssm_a_re, ssm_a_im, ssm_log_dt, ssm_b_re, ssm_b_im, ssm_c_re, ssm_c_im, ssm_d, w_glu, b_glu, w_out, ln1_g, ln1_b, w_router, b_router, w_gate, w_up, w_down, ws_gate, ws_up, ws_down, w_pg, b_pg, w_pp, ln2_g, ln2_b):
    depth = w_in.shape[0]
    alpha = (2.0 * depth) ** 0.25
    nb_p, nb_s = x_prompt.shape[0], x_sample.shape[0]
    d_pool = state_pool.shape[-1]
    groups, n_st = state_ssm_re.shape[-2:]
    n_state = groups * n_st

    wts = _prepare_weights(ln_in_g, ln_in_b, w_in, w_pool, pool_scale, ssm_a_re, ssm_a_im, ssm_log_dt,
                           ssm_b_re, ssm_b_im, ssm_c_re, ssm_c_im, ssm_d, w_glu, b_glu, w_out, ln1_g, ln1_b,
                           w_router, b_router, w_gate, w_up, w_down, ws_gate, ws_up, ws_down,
                           w_pg, b_pg, w_pp, ln2_g, ln2_b)

    hp = x_prompt
    pp = p_prompt
    hs = _time_major(x_sample)
    ps = p_sample.transpose(0, 2, 1, 3).reshape(depth, hs.shape[0], p_sample.shape[-1])
    pool0_p = jnp.zeros((POOL_BUF * nb_p, d_pool), F32)
    s0_p = jnp.zeros((nb_p, n_state), F32)
    pool0_s = state_pool.transpose(0, 2, 1, 3).reshape(depth, POOL_BUF * nb_s, d_pool)
    s0re_s = state_ssm_re.reshape(depth, nb_s, n_state)
    s0im_s = state_ssm_im.reshape(depth, nb_s, n_state)

    outs = {k: [] for k in ("pool_p", "re_p", "im_p", "pool_s", "re_s", "im_s")}
    for l in range(depth):
        hp, bp, rp, ip = _mixer(l, l == 0, alpha, nb_p, 0, hp, pool0_p, s0_p, s0_p, wts)
        hp = _ffn(l, alpha, hp, pp, wts, out_batch=nb_p if l == depth - 1 else None)
        hs, bs, rs, is_ = _mixer(l, l == 0, alpha, nb_s, PAST_LEN, hs, pool0_s[l], s0re_s[l], s0im_s[l], wts)
        hs = _ffn(l, alpha, hs, ps, wts)
        outs["pool_p"].append(_batch_major(bp, nb_p))
        outs["re_p"].append(rp.reshape(nb_p, groups, n_st))
        outs["im_p"].append(ip.reshape(nb_p, groups, n_st))
        outs["pool_s"].append(_batch_major(bs, nb_s))
        outs["re_s"].append(rs.reshape(nb_s, groups, n_st))
        outs["im_s"].append(is_.reshape(nb_s, groups, n_st))

    return (hp, _batch_major(hs, nb_s),
            jnp.stack(outs["pool_p"]), jnp.stack(outs["re_p"]), jnp.stack(outs["im_p"]),
            jnp.stack(outs["pool_s"]), jnp.stack(outs["re_s"]), jnp.stack(outs["im_s"]))
```
